```python
import math
import jax, jax.numpy as jnp
from jax import lax
import numpy as np


D_MODEL = 1024
BATCH = 8
SEQ = 8192
DEPTH = 1
DEC_BATCH = 128
DEC_SEQ = 8
PAST_LEN = 8192
PAGE_SIZE = 128

N_HEADS = 8
HEAD_DIM = 64
D_ATTN = N_HEADS * HEAD_DIM
D_RNN = D_MODEL // 2
N_RNN_BLOCKS = 8
RNN_BLOCK = D_RNN // N_RNN_BLOCKS
CONV_WIDTH = 4
LRU_C = 8.0
D_MIX = D_ATTN + D_RNN
IN_COLS = 3 * D_ATTN + 2 * D_RNN
D_FF = ((8 * D_MODEL // 3 + 255) // 256) * 256
DILATED = ((128, 1), (512, 4), (2048, 16))
MAX_WINDOW = 2048
BLK = 128
EPS = 1e-6

kernel_name = 'dilated_swa_rglru_macaron_hybrid'


def rmsnorm(x, g):
    xf = x.astype(jnp.float32)
    y = xf * lax.rsqrt(jnp.mean(xf * xf, axis=-1, keepdims=True) + EPS)
    return (y * g.astype(jnp.float32)).astype(x.dtype)


def swiglu(x, wg, wu, wd):
    return (jax.nn.silu(x @ wg) * (x @ wu)) @ wd


def _fold(t, dil, sp):
    b, s = t.shape[:2]
    rest = t.shape[2:]
    t = jnp.pad(t, [(0, 0), (0, sp - s)] + [(0, 0)] * len(rest))
    t = jnp.moveaxis(t.reshape(b, sp // dil, dil, *rest), 2, 1)
    return t.reshape(b, dil, sp // (dil * BLK), BLK, *rest)


def _unfold(t, s):
    b, dil, nb, blk = t.shape[:4]
    rest = t.shape[4:]
    t = jnp.moveaxis(t.reshape(b, dil, nb * blk, *rest), 1, 2)
    return t.reshape(b, dil * nb * blk, *rest)[:, :s]


def branch_prompt(q, k, v, win, dil):
    b, s, h, c = q.shape
    span = dil * BLK
    sp = -(-s // span) * span
    nb = sp // span
    qb, kb, vb = _fold(q, dil, sp), _fold(k, dil, sp), _fold(v, dil, sp)

    def with_prev(t):
        prev = jnp.pad(t[:, :, :-1], ((0, 0), (0, 0), (1, 0), (0, 0), (0, 0), (0, 0)))
        return jnp.concatenate([prev, t], axis=3)

    kk, vv = with_prev(kb), with_prev(vb)
    sc = jnp.einsum('bgnqhc,bgnkhc->bgnhqk', qb, kk, preferred_element_type=jnp.float32)
    qi = jnp.arange(BLK)[:, None]
    kj = jnp.arange(2 * BLK)[None, :]
    dist = BLK + qi - kj
    band = (dist >= 0) & (dist <= win // dil)
    has_prev = (jnp.arange(nb)[:, None, None] > 0) | (kj[None] >= BLK)
    valid = band[None] & has_prev
    sc = jnp.where(valid[None, None, :, None], sc, -jnp.inf)
    m = sc.max(-1)
    p = jnp.exp(sc - m[..., None])
    den = p.sum(-1)
    o = jnp.einsum('bgnhqk,bgnkhc->bgnqhc', p, vv.astype(jnp.float32))
    m = jnp.moveaxis(m, 3, 4)
    den = jnp.moveaxis(den, 3, 4)
    return _unfold(o, s), _unfold(m, s), _unfold(den, s)


def branch_sample(q, kcat, vcat, win, dil):
    t = q.shape[1]
    past = kcat.shape[1] - t
    idx = past + jnp.arange(t)[:, None] - dil * jnp.arange(win // dil + 1)[None, :]
    valid = idx >= 0
    idx = jnp.maximum(idx, 0)
    kg = kcat[:, idx]
    vg = vcat[:, idx]
    sc = jnp.einsum('bthc,btjhc->bthj', q, kg, preferred_element_type=jnp.float32)
    sc = jnp.where(valid[None, :, None, :], sc, -jnp.inf)
    m = sc.max(-1)
    p = jnp.exp(sc - m[..., None])
    den = p.sum(-1)
    o = jnp.einsum('bthj,btjhc->bthc', p, vg.astype(jnp.float32))
    return o, m, den


def merge_branches(parts):
    o = jnp.stack([pt[0] for pt in parts])
    m = jnp.stack([pt[1] for pt in parts])
    d = jnp.stack([pt[2] for pt in parts])
    w = jnp.exp(m - m.max(0, keepdims=True))
    return (w[..., None] * o).sum(0) / (w * d).sum(0)[..., None]


def conv_rglru(u, conv_buf, h0, conv_w, conv_b, w_a, b_a, w_x, b_x, lam):
    b, t, r = u.shape
    full = jnp.concatenate([conv_buf.astype(u.dtype), u], axis=1)
    xc = conv_b + conv_w[0] * full[:, 0:t]
    for j in range(1, CONV_WIDTH):
        xc = xc + conv_w[j] * full[:, j:j + t]
    new_buf = full[:, t:]
    xf = xc.astype(jnp.float32)
    xb = xf.reshape(b, t, N_RNN_BLOCKS, RNN_BLOCK)
    rg = jax.nn.sigmoid(jnp.einsum('btnc,ncd->btnd', xb, w_a.astype(jnp.float32)) + b_a.astype(jnp.float32)).reshape(b, t, r)
    ig = jax.nn.sigmoid(jnp.einsum('btnc,ncd->btnd', xb, w_x.astype(jnp.float32)) + b_x.astype(jnp.float32)).reshape(b, t, r)
    log_a = -LRU_C * rg * jax.nn.softplus(-lam.astype(jnp.float32))
    a = jnp.exp(log_a)
    bt = jnp.sqrt(-jnp.expm1(2.0 * log_a)) * ig * xf
    bt = bt.at[:, 0].add(a[:, 0] * h0.astype(jnp.float32))

    def comb(lhs, rhs):
        a1, b1 = lhs
        a2, b2 = rhs
        return a1 * a2, a2 * b1 + b2

    _, hs = lax.associative_scan(comb, (a, bt), axis=1)
    return hs.astype(u.dtype), hs[:, -1].astype(u.dtype), new_buf


def decoder_layer(x, conv_buf, h0, k_past, v_past, lw):
    (ln1, w1g, w1u, w1d, ln_m, w_in, conv_w, conv_b, w_a, b_a, w_x, b_x, lam,
     w_out, ln2, w2g, w2u, w2d) = lw
    b, t, _ = x.shape
    x = x + 0.5 * swiglu(rmsnorm(x, ln1), w1g, w1u, w1d)
    z = rmsnorm(x, ln_m) @ w_in
    q, k, v, u, g = jnp.split(z, [D_ATTN, 2 * D_ATTN, 3 * D_ATTN, 3 * D_ATTN + D_RNN], axis=-1)
    q = q.reshape(b, t, N_HEADS, HEAD_DIM) * (HEAD_DIM ** -0.5)
    k = k.reshape(b, t, N_HEADS, HEAD_DIM)
    v = v.reshape(b, t, N_HEADS, HEAD_DIM)
    if k_past is None:
        parts = [branch_prompt(q, k, v, win, dil) for win, dil in DILATED]
        keep = min(MAX_WINDOW, t)
        k_state, v_state = k[:, t - keep:], v[:, t - keep:]
    else:
        kcat = jnp.concatenate([k_past.astype(k.dtype), k], axis=1)
        vcat = jnp.concatenate([v_past.astype(v.dtype), v], axis=1)
        parts = [branch_sample(q, kcat, vcat, win, dil) for win, dil in DILATED]
        k_state, v_state = k, v
    attn = merge_branches(parts).astype(x.dtype).reshape(b, t, D_ATTN)
    rnn, h_last, new_buf = conv_rglru(u, conv_buf, h0, conv_w, conv_b, w_a, b_a, w_x, b_x, lam)
    x = x + jnp.concatenate([attn, rnn * jax.nn.gelu(g)], axis=-1) @ w_out
    x = x + 0.5 * swiglu(rmsnorm(x, ln2), w2g, w2u, w2d)
    return x, k_state, v_state, h_last, new_buf


def setup_inputs(seed: int = 0) -> dict:
    key = jax.random.key(seed)
    ks = jax.random.split(key, 32)
    w_buf = min(MAX_WINDOW, PAST_LEN)

    def nrm(k, shape, scale):
        return jax.random.normal(k, shape, jnp.float32) * scale

    a0 = jax.random.uniform(ks[18], (DEPTH, D_RNN), jnp.float32, 0.9, 0.999)
    sig = a0 ** (1.0 / LRU_C)
    lru_lambda = jnp.log(sig) - jnp.log1p(-sig)
    return {
        'x_prompt': nrm(ks[0], (BATCH, SEQ, D_MODEL), 1.0),
        'x_sample': nrm(ks[1], (DEC_BATCH, DEC_SEQ, D_MODEL), 1.0),
        'cache_k_win': nrm(ks[2], (DEPTH, DEC_BATCH, w_buf, N_HEADS, HEAD_DIM), 1.0),
        'cache_v_win': nrm(ks[3], (DEPTH, DEC_BATCH, w_buf, N_HEADS, HEAD_DIM), 1.0),
        'state_lru_h': nrm(ks[4], (DEPTH, DEC_BATCH, D_RNN), 0.5),
        'state_lru_conv': nrm(ks[5], (DEPTH, DEC_BATCH, CONV_WIDTH - 1, D_RNN), 1.0),
        'ln_ffn1': 1.0 + nrm(ks[6], (DEPTH, D_MODEL), 0.02),
        'w_ffn1_gate': nrm(ks[7], (DEPTH, D_MODEL, D_FF), D_MODEL ** -0.5),
        'w_ffn1_up': nrm(ks[8], (DEPTH, D_MODEL, D_FF), D_MODEL ** -0.5),
        'w_ffn1_down': nrm(ks[9], (DEPTH, D_FF, D_MODEL), D_FF ** -0.5),
        'ln_mix': 1.0 + nrm(ks[10], (DEPTH, D_MODEL), 0.02),
        'w_in': nrm(ks[11], (DEPTH, D_MODEL, IN_COLS), D_MODEL ** -0.5),
        'conv_w': nrm(ks[12], (DEPTH, CONV_WIDTH, D_RNN), CONV_WIDTH ** -0.5),
        'conv_b': nrm(ks[13], (DEPTH, D_RNN), 0.01),
        'w_gate_a': nrm(ks[14], (DEPTH, N_RNN_BLOCKS, RNN_BLOCK, RNN_BLOCK), RNN_BLOCK ** -0.5),
        'b_gate_a': nrm(ks[15], (DEPTH, N_RNN_BLOCKS, RNN_BLOCK), 0.01),
        'w_gate_x': nrm(ks[16], (DEPTH, N_RNN_BLOCKS, RNN_BLOCK, RNN_BLOCK), RNN_BLOCK ** -0.5),
        'b_gate_x': nrm(ks[17], (DEPTH, N_RNN_BLOCKS, RNN_BLOCK), 0.01),
        'lru_lambda': lru_lambda,
        'w_out': nrm(ks[19], (DEPTH, D_MIX, D_MODEL), D_MIX ** -0.5),
        'ln_ffn2': 1.0 + nrm(ks[20], (DEPTH, D_MODEL), 0.02),
        'w_ffn2_gate': nrm(ks[21], (DEPTH, D_MODEL, D_FF), D_MODEL ** -0.5),
        'w_ffn2_up': nrm(ks[22], (DEPTH, D_MODEL, D_FF), D_MODEL ** -0.5),
        'w_ffn2_down': nrm(ks[23], (DEPTH, D_FF, D_MODEL), D_FF ** -0.5),
        'ln_final': 1.0 + nrm(ks[24], (D_MODEL,), 0.02),
    }


def reference(x_prompt, x_sample, cache_k_win, cache_v_win, state_lru_h, state_lru_conv,
              ln_ffn1, w_ffn1_gate, w_ffn1_up, w_ffn1_down, ln_mix, w_in, conv_w, conv_b,
              w_gate_a, b_gate_a, w_gate_x, b_gate_x, lru_lambda, w_out,
              ln_ffn2, w_ffn2_gate, w_ffn2_up, w_ffn2_down, ln_final):
    xp, xs = x_prompt, x_sample
    bp = xp.shape[0]
    kp_l, vp_l, hp_l, cp_l = [], [], [], []
    ks_l, vs_l, hs_l, cs_l = [], [], [], []
    for l in range(DEPTH):
        lw = (ln_ffn1[l], w_ffn1_gate[l], w_ffn1_up[l], w_ffn1_down[l], ln_mix[l], w_in[l],
              conv_w[l], conv_b[l], w_gate_a[l], b_gate_a[l], w_gate_x[l], b_gate_x[l],
              lru_lambda[l], w_out[l], ln_ffn2[l], w_ffn2_gate[l], w_ffn2_up[l], w_ffn2_down[l])
        zero_buf = jnp.zeros((bp, CONV_WIDTH - 1, D_RNN), xp.dtype)
        zero_h = jnp.zeros((bp, D_RNN), xp.dtype)
        xp, kp, vp, hp, cp = decoder_layer(xp, zero_buf, zero_h, None, None, lw)
        xs, kn, vn, hn, cn = decoder_layer(xs, state_lru_conv[l], state_lru_h[l],
                                           cache_k_win[l], cache_v_win[l], lw)
        kp_l.append(kp); vp_l.append(vp); hp_l.append(hp); cp_l.append(cp)
        ks_l.append(kn); vs_l.append(vn); hs_l.append(hn); cs_l.append(cn)
    y_prompt = rmsnorm(xp, ln_final)
    y_sample = rmsnorm(xs, ln_final)
    return (y_prompt, y_sample,
            jnp.stack(kp_l), jnp.stack(vp_l), jnp.stack(hp_l), jnp.stack(cp_l),
            jnp.stack(ks_l), jnp.stack(vs_l), jnp.stack(hs_l), jnp.stack(cs_l))
```

```python
import functools

import jax
import jax.numpy as jnp
import numpy as np
from jax.experimental import pallas as pl
from jax.experimental.pallas import tpu as pltpu

N_HEADS = 8
HEAD_DIM = 64
D_ATTN = N_HEADS * HEAD_DIM
CONV_WIDTH = 4
LRU_C = 8.0
DILATED = ((128, 1), (512, 4), (2048, 16))
MAX_WINDOW = 2048
BLK = 128
EPS = 1e-6

LANES = 128
SUBLANES = 8
VMEM_LIMIT = 56 * 1024 * 1024

F32 = jnp.float32
BF16 = jnp.bfloat16
NEG_INF = float("-inf")


def _rms(x, g):
    return x * jax.lax.rsqrt(jnp.mean(x * x, axis=-1, keepdims=True) + EPS) * g


def _swiglu(xn, wg_ref, wu_ref, wd_ref):
    gate = jnp.dot(xn, wg_ref[...], preferred_element_type=F32)
    up = jnp.dot(xn, wu_ref[...], preferred_element_type=F32)
    h = (gate * jax.nn.sigmoid(gate) * up).astype(BF16)
    return jnp.dot(h, wd_ref[...], preferred_element_type=F32)


def _resident(shape):
    return pl.BlockSpec(shape, lambda *_: (0,) * len(shape), pipeline_mode=pl.Buffered(1))


def _ffn_in_kernel(x_ref, ln1_ref, wg_ref, wu_ref, wd_ref, lnm_ref, win_ref,
                   x1_ref, q_ref, k_ref, v_ref, kf_ref, vf_ref, u_ref, g_ref):
    x = x_ref[...]
    xn = _rms(x, ln1_ref[...]).astype(BF16)
    x1 = x + 0.5 * _swiglu(xn, wg_ref, wu_ref, wd_ref)
    x1_ref[...] = x1
    zn = _rms(x1, lnm_ref[...]).astype(BF16)
    z = jnp.dot(zn, win_ref[...], preferred_element_type=F32)
    d = D_ATTN
    q_ref[...] = (z[:, :d] * (HEAD_DIM ** -0.5)).astype(BF16)
    k = z[:, d:2 * d]
    v = z[:, 2 * d:3 * d]
    k_ref[...] = k.astype(BF16)
    v_ref[...] = v.astype(BF16)
    kf_ref[...] = k
    vf_ref[...] = v
    r = u_ref.shape[-1]
    u_ref[...] = z[:, 3 * d:3 * d + r]
    g_ref[...] = z[:, 3 * d + r:]


def _ffn_in(x, ln1, wg, wu, wd, lnm, win, *, seq, keep, tm):
    n, dm = x.shape
    dff = wg.shape[1]
    d_rnn = (win.shape[1] - 3 * D_ATTN) // 2
    tiles_per_seq = seq // tm
    keep_tiles = keep // tm
    first_kept = tiles_per_seq - keep_tiles

    def kept_map(i):
        return ((i // tiles_per_seq) * keep_tiles + jnp.maximum(i % tiles_per_seq - first_kept, 0), 0)

    row = lambda w: pl.BlockSpec((tm, w), lambda i: (i, 0))
    n_keep = (n // seq) * keep
    return pl.pallas_call(
        _ffn_in_kernel,
        grid=(n // tm,),
        in_specs=[row(dm), _resident((1, dm)), _resident((dm, dff)), _resident((dm, dff)),
                  _resident((dff, dm)), _resident((1, dm)), _resident(win.shape)],
        out_specs=[row(dm), row(D_ATTN), row(D_ATTN), row(D_ATTN),
                   pl.BlockSpec((tm, D_ATTN), kept_map), pl.BlockSpec((tm, D_ATTN), kept_map),
                   row(d_rnn), row(d_rnn)],
        out_shape=[jax.ShapeDtypeStruct((n, dm), F32),
                   jax.ShapeDtypeStruct((n, D_ATTN), BF16),
                   jax.ShapeDtypeStruct((n, D_ATTN), BF16),
                   jax.ShapeDtypeStruct((n, D_ATTN), BF16),
                   jax.ShapeDtypeStruct((n_keep, D_ATTN), F32),
                   jax.ShapeDtypeStruct((n_keep, D_ATTN), F32),
                   jax.ShapeDtypeStruct((n, d_rnn), F32),
                   jax.ShapeDtypeStruct((n, d_rnn), F32)],
        compiler_params=pltpu.CompilerParams(dimension_semantics=("arbitrary",),
                                             vmem_limit_bytes=VMEM_LIMIT),
        name="ffn_in",
    )(x, ln1, wg, wu, wd, lnm, win)


def _attn_branch_kernel(q_ref, kc_ref, kp_ref, vc_ref, vp_ref, o_ref, lse_ref, kbuf, vbuf):
    i = pl.program_id(2)
    mq = q_ref.shape[1]
    kbuf[0:BLK] = kp_ref[0]
    kbuf[BLK:] = kc_ref[0]
    vbuf[0:BLK] = vp_ref[0]
    vbuf[BLK:] = vc_ref[0]
    qi = jax.lax.broadcasted_iota(jnp.int32, (BLK, 2 * BLK), 0)
    kk = jax.lax.broadcasted_iota(jnp.int32, (BLK, 2 * BLK), 1)
    band = (kk >= qi) & (kk <= qi + BLK)
    band_first = band & ((kk >= BLK) | (i > 0))
    bias = jnp.where(band, 0.0, NEG_INF).astype(F32)
    bias_first = jnp.where(band_first, 0.0, NEG_INF).astype(F32)
    lane = jax.lax.broadcasted_iota(jnp.int32, (BLK, LANES), 1)
    low = lane < HEAD_DIM
    nt = (((1,), (1,)), ((), ()))
    for j in range(mq // BLK):
        bj = bias_first if j == 0 else bias
        rows = slice(j * BLK, (j + 1) * BLK)
        krows = slice(j * BLK, (j + 2) * BLK)
        for hp in range(D_ATTN // LANES):
            cols = slice(hp * LANES, (hp + 1) * LANES)
            q2 = q_ref[0, rows, cols]
            k2 = kbuf[krows, cols]
            v2 = vbuf[krows, cols]
            outs = []
            for sel in (low, ~low):
                qm = jnp.where(sel, q2, jnp.zeros_like(q2))
                s = jax.lax.dot_general(qm, k2, nt, preferred_element_type=F32) + bj
                m = jnp.max(s, axis=-1, keepdims=True)
                p = jnp.exp(s - m)
                l = jnp.sum(p, axis=-1, keepdims=True)
                o = jnp.dot(p.astype(BF16), v2, preferred_element_type=F32)
                outs.append((o * (1.0 / l), m + jnp.log(l)))
            o_ref[0, rows, cols] = jnp.where(low, outs[0][0], outs[1][0])
            lse_ref[0, rows, cols] = jnp.where(low, outs[0][1], outs[1][1])


def _attn_branch(q, k, v, dil, *, mq):
    b, s, d = q.shape
    sub = s // dil
    mq = min(mq, sub)
    fold = lambda t: t.reshape(b, sub, dil * d)
    cur = pl.BlockSpec((1, mq, d), lambda bi, r, i: (bi, i, r))
    prev = pl.BlockSpec((1, BLK, d), lambda bi, r, i: (bi, jnp.maximum(i * (mq // BLK) - 1, 0), r))
    o, lse = pl.pallas_call(
        _attn_branch_kernel,
        grid=(b, dil, sub // mq),
        in_specs=[cur, cur, prev, cur, prev],
        out_specs=[cur, cur],
        out_shape=[jax.ShapeDtypeStruct((b, sub, dil * d), F32)] * 2,
        scratch_shapes=[pltpu.VMEM((BLK + mq, d), BF16), pltpu.VMEM((BLK + mq, d), BF16)],
        compiler_params=pltpu.CompilerParams(
            dimension_semantics=("arbitrary", "arbitrary", "arbitrary"), vmem_limit_bytes=VMEM_LIMIT),
        name=f"attn_dil{dil}",
    )(fold(q), fold(k), fold(k), fold(v), fold(v))
    return o.reshape(b, s, d), lse.reshape(b, s, d)


def _branch_counts(t, w_buf):
    i = np.arange(t)[:, None]
    c = np.arange(w_buf)[None, :]
    n = np.arange(2 * SUBLANES)[None, :]
    cnt_c = np.zeros((t, w_buf), np.float32)
    cnt_n = np.zeros((t, 2 * SUBLANES), np.float32)
    for win, dil in DILATED:
        dist = w_buf + i - c
        cnt_c += ((dist % dil == 0) & (dist >= dil) & (dist <= win)).astype(np.float32)
        dist = i - n
        cnt_n += ((n < t) & (dist >= 0) & (dist % dil == 0) & (dist <= win)).astype(np.float32)
    return np.tile(cnt_c, (N_HEADS, 1)), np.tile(cnt_n, (N_HEADS, 1))


def _attn_sample_kernel(q_ref, kn_ref, vn_ref, kc_ref, vc_ref, cc_ref, cn_ref, o_ref):
    t = q_ref.shape[1]
    q = q_ref[0]
    rows = N_HEADS * t
    q_all = jnp.concatenate([q] * N_HEADS, axis=0)
    row_head = jax.lax.broadcasted_iota(jnp.int32, (rows, D_ATTN), 0) // t
    col_head = jax.lax.broadcasted_iota(jnp.int32, (rows, D_ATTN), 1) // HEAD_DIM
    own = row_head == col_head
    qbd = jnp.where(own, q_all, 0.0).astype(BF16)
    pad = jnp.zeros((2 * SUBLANES - t, D_ATTN), F32)
    kn = jnp.concatenate([kn_ref[0], pad], axis=0).astype(BF16)
    vn = jnp.concatenate([vn_ref[0], pad], axis=0).astype(BF16)
    kc = kc_ref[0].astype(BF16)
    vc = vc_ref[0].astype(BF16)
    nt = (((1,), (1,)), ((), ()))
    s_c = jax.lax.dot_general(qbd, kc, nt, preferred_element_type=F32)
    s_n = jax.lax.dot_general(qbd, kn, nt, preferred_element_type=F32)
    cnt_c = cc_ref[...]
    cnt_n = cn_ref[...]
    s_c = jnp.where(cnt_c > 0.0, s_c, NEG_INF)
    s_n = jnp.where(cnt_n > 0.0, s_n, NEG_INF)
    m = jnp.maximum(jnp.max(s_c, axis=-1, keepdims=True), jnp.max(s_n, axis=-1, keepdims=True))
    p_c = cnt_c * jnp.exp(s_c - m)
    p_n = cnt_n * jnp.exp(s_n - m)
    l = jnp.sum(p_c, axis=-1, keepdims=True) + jnp.sum(p_n, axis=-1, keepdims=True)
    o = (jnp.dot(p_c.astype(BF16), vc, preferred_element_type=F32)
         + jnp.dot(p_n.astype(BF16), vn, preferred_element_type=F32)) * (1.0 / l)
    o = jnp.where(own, o, 0.0)
    acc = o[0:t]
    for h in range(1, N_HEADS):
        acc = acc + o[h * t:(h + 1) * t]
    o_ref[0] = acc


def _attn_sample(q, k_new, v_new, k_cache, v_cache):
    b, t, d = q.shape
    w_buf = k_cache.shape[1]
    cnt_c, cnt_n = _branch_counts(t, w_buf)
    tok = pl.BlockSpec((1, t, d), lambda bi: (bi, 0, 0))
    cache = pl.BlockSpec((1, w_buf, d), lambda bi: (bi, 0, 0))
    return pl.pallas_call(
        _attn_sample_kernel,
        grid=(b,),
        in_specs=[tok, tok, tok, cache, cache, _resident(cnt_c.shape), _resident(cnt_n.shape)],
        out_specs=tok,
        out_shape=jax.ShapeDtypeStruct((b, t, d), F32),
        compiler_params=pltpu.CompilerParams(dimension_semantics=("arbitrary",),
                                             vmem_limit_bytes=VMEM_LIMIT),
        name="attn_sample",
    )(q, k_new, v_new, k_cache, v_cache, jnp.asarray(cnt_c), jnp.asarray(cnt_n))


def _rglru_kernel(u_ref, g_ref, cbuf_ref, h0_ref, cw_ref, cb_ref, wa_ref, ba_ref, wx_ref, bx_ref,
                  lam_ref, r_ref, hlast_ref, tail_ref, ubuf, a_s, b_s, hcar):
    tc = u_ref.shape[1]
    pad = SUBLANES

    @pl.when(pl.program_id(1) == 0)
    def _():
        ubuf[0:pad] = cbuf_ref[0]
        hcar[...] = h0_ref[0]

    ubuf[pad:pad + tc] = u_ref[0]
    first = pad - (CONV_WIDTH - 1)
    xc = cb_ref[...] + cw_ref[0:1, :] * ubuf[first:first + tc]
    for j in range(1, CONV_WIDTH):
        xc = xc + cw_ref[j:j + 1, :] * ubuf[first + j:first + j + tc]
    new_tail = ubuf[tc:tc + pad]
    ubuf[0:pad] = new_tail
    tail_ref[0] = new_tail

    xb = xc.astype(BF16)
    rg = jax.nn.sigmoid(jnp.dot(xb, wa_ref[...], preferred_element_type=F32) + ba_ref[...])
    ig = jax.nn.sigmoid(jnp.dot(xb, wx_ref[...], preferred_element_type=F32) + bx_ref[...])
    log_a = (-LRU_C * jax.nn.softplus(-lam_ref[...])) * rg
    a = jnp.exp(log_a)
    a_s[...] = a
    b_s[...] = jnp.sqrt(-jnp.tanh(log_a) * (a * a + 1.0)) * ig * xc

    row = jax.lax.broadcasted_iota(jnp.int32, (SUBLANES, a_s.shape[1]), 0)

    def group(gi, h):
        r0 = pl.multiple_of(gi * SUBLANES, SUBLANES)
        a = a_s[pl.ds(r0, SUBLANES), :]
        b = b_s[pl.ds(r0, SUBLANES), :]
        for sh in (1, 2, 4):
            a_prev = pltpu.roll(a, sh, axis=0)
            b_prev = pltpu.roll(b, sh, axis=0)
            take = row >= sh
            b = jnp.where(take, a * b_prev + b, b)
            a = jnp.where(take, a * a_prev, a)
        hs = a * h + b
        b_s[pl.ds(r0, SUBLANES), :] = hs
        return hs[SUBLANES - 1:SUBLANES, :]

    h_end = jax.lax.fori_loop(0, tc // SUBLANES, group, hcar[...])
    hcar[...] = h_end
    hlast_ref[0] = h_end
    r_ref[0] = (b_s[...] * jax.nn.gelu(g_ref[0])).astype(BF16)


def _rglru(u, g, conv_buf, h0, conv_w, conv_b, wa_bd, b_a, wx_bd, b_x, lam, *, tc):
    b, t, r = u.shape
    tc = min(tc, t)
    pad = SUBLANES
    cbuf = jnp.pad(conv_buf, ((0, 0), (pad - (CONV_WIDTH - 1), 0), (0, 0)))
    chunk = pl.BlockSpec((1, tc, r), lambda bi, ti: (bi, ti, 0))
    per_seq = lambda rows: pl.BlockSpec((1, rows, r), lambda bi, ti: (bi, 0, 0))
    vec = _resident((1, r))
    out, h_last, tail = pl.pallas_call(
        _rglru_kernel,
        grid=(b, t // tc),
        in_specs=[chunk, chunk, per_seq(pad), per_seq(1), _resident((CONV_WIDTH, r)), vec,
                  _resident((r, r)), vec, _resident((r, r)), vec, vec],
        out_specs=[chunk, per_seq(1), per_seq(pad)],
        out_shape=[jax.ShapeDtypeStruct((b, t, r), BF16),
                   jax.ShapeDtypeStruct((b, 1, r), F32),
                   jax.ShapeDtypeStruct((b, pad, r), F32)],
        scratch_shapes=[pltpu.VMEM((pad + tc, r), F32), pltpu.VMEM((tc, r), F32),
                        pltpu.VMEM((tc, r), F32), pltpu.VMEM((1, r), F32)],
        compiler_params=pltpu.CompilerParams(dimension_semantics=("arbitrary", "arbitrary"),
                                             vmem_limit_bytes=VMEM_LIMIT),
        name="rglru",
    )(u, g, cbuf, h0.reshape(b, 1, r), conv_w, conv_b.reshape(1, r), wa_bd, b_a.reshape(1, r),
      wx_bd, b_x.reshape(1, r), lam.reshape(1, r))
    return out, h_last.reshape(b, r), tail[:, pad - (CONV_WIDTH - 1):]


def _out_ffn_kernel(*refs, n_branches, final_norm):
    x1_ref = refs[0]
    br = refs[1:1 + 2 * n_branches]
    (attn_ref, r_ref, wo_ref, ln2_ref, wg_ref, wu_ref, wd_ref, lnf_ref, y_ref) = refs[1 + 2 * n_branches:]
    if n_branches:
        lses = [br[2 * i + 1][...] for i in range(n_branches)]
        m = functools.reduce(jnp.maximum, lses)
        ws = [jnp.exp(l - m) for l in lses]
        num = functools.reduce(jnp.add, [w * br[2 * i][...] for i, w in enumerate(ws)])
        attn = num * (1.0 / functools.reduce(jnp.add, ws))
    else:
        attn = attn_ref[...]
    mix = jnp.concatenate([attn.astype(BF16), r_ref[...]], axis=-1)
    x2 = x1_ref[...] + jnp.dot(mix, wo_ref[...], preferred_element_type=F32)
    xn = _rms(x2, ln2_ref[...]).astype(BF16)
    x3 = x2 + 0.5 * _swiglu(xn, wg_ref, wu_ref, wd_ref)
    y_ref[...] = _rms(x3, lnf_ref[...]) if final_norm else x3


def _out_ffn(x1, branches, attn, r, wo, ln2, wg, wu, wd, lnf, *, final_norm, tm):
    n, dm = x1.shape
    dff = wg.shape[1]
    row = lambda w: pl.BlockSpec((tm, w), lambda i: (i, 0))
    flat = [t for pair in branches for t in pair]
    if attn is None:
        attn = jnp.zeros((tm, D_ATTN), F32)
        attn_spec = _resident(attn.shape)
    else:
        attn_spec = row(D_ATTN)
    return pl.pallas_call(
        functools.partial(_out_ffn_kernel, n_branches=len(branches), final_norm=final_norm),
        grid=(n // tm,),
        in_specs=[row(dm)] + [row(D_ATTN)] * len(flat)
                 + [attn_spec, row(r.shape[1]), _resident(wo.shape), _resident((1, dm)),
                    _resident((dm, dff)), _resident((dm, dff)), _resident((dff, dm)), _resident((1, dm))],
        out_specs=row(dm),
        out_shape=jax.ShapeDtypeStruct((n, dm), F32),
        compiler_params=pltpu.CompilerParams(dimension_semantics=("arbitrary",),
                                             vmem_limit_bytes=VMEM_LIMIT),
        name="out_ffn",
    )(x1, *flat, attn, r, wo, ln2, wg, wu, wd, lnf)


def _block_diag(w):
    nb, c, d = w.shape
    return jnp.einsum("ncd,nm->ncmd", w, jnp.eye(nb, dtype=w.dtype)).reshape(nb * c, nb * d)


def _layer(x, conv_buf, h0, k_past, v_past, lw, lnf, *, final_norm, tm, mq, tc):
    (ln1, w1g, w1u, w1d, ln_m, w_in, conv_w, conv_b, wa_bd, b_a, wx_bd, b_x, lam,
     w_out, ln2, w2g, w2u, w2d) = lw
    b, t, dm = x.shape
    r = conv_w.shape[-1]
    prompt = k_past is None
    keep = min(MAX_WINDOW, t) if prompt else t
    xf = x.reshape(b * t, dm)
    if prompt:
        seq, keep_rows = t, keep
    else:
        seq, keep_rows = b * t, b * t
    x1, q, k, v, kf, vf, u, g = _ffn_in(xf, ln1, w1g, w1u, w1d, ln_m, w_in,
                                        seq=seq, keep=keep_rows, tm=tm)
    shp = lambda a: a.reshape(b, t, a.shape[-1])
    if prompt:
        branches = [_attn_branch(shp(q), shp(k), shp(v), dil, mq=mq) for _, dil in DILATED]
        branches = [(o.reshape(b * t, D_ATTN), l.reshape(b * t, D_ATTN)) for o, l in branches]
        attn = None
    else:
        branches = []
        attn = _attn_sample(shp(q).astype(F32), shp(kf), shp(vf),
                            k_past.reshape(b, -1, D_ATTN), v_past.reshape(b, -1, D_ATTN))
        attn = attn.reshape(b * t, D_ATTN)
    rnn, h_last, new_buf = _rglru(shp(u), shp(g), conv_buf, h0, conv_w, conv_b, wa_bd, b_a, wx_bd,
                                  b_x, lam, tc=tc)
    y = _out_ffn(x1, branches, attn, rnn.reshape(b * t, r), w_out, ln2, w2g, w2u, w2d, lnf,
                 final_norm=final_norm, tm=tm)
    k_state = kf.reshape(b, keep, N_HEADS, HEAD_DIM)
    v_state = vf.reshape(b, keep, N_HEADS, HEAD_DIM)
    return y.reshape(b, t, dm), k_state, v_state, h_last, new_buf


def kernel(x_prompt, x_sample, cache_k_win, cache_v_win, state_lru_h, state_lru_conv, ln_ffn1, w_ffn1_gate, w_ffn1_up, w_ffn1_down, ln_mix, w_in, conv_w, conv_b, w_gate_a, b_gate_a, w_gate_x, b_gate_x, lru_lambda, w_out, ln_ffn2, w_ffn2_gate, w_ffn2_up, w_ffn2_down, ln_final):
    depth = ln_ffn1.shape[0]
    dm = x_prompt.shape[-1]
    d_rnn = conv_w.shape[-1]
    bp = x_prompt.shape[0]
    xp, xs = x_prompt, x_sample
    outs = [[] for _ in range(8)]
    lnf = ln_final.reshape(1, dm)
    cfg = dict(tm=256, mq=512, tc=512)
    for l in range(depth):
        lw = (ln_ffn1[l].reshape(1, dm), w_ffn1_gate[l].astype(BF16), w_ffn1_up[l].astype(BF16),
              w_ffn1_down[l].astype(BF16), ln_mix[l].reshape(1, dm), w_in[l].astype(BF16),
              conv_w[l], conv_b[l], _block_diag(w_gate_a[l]).astype(BF16), b_gate_a[l].reshape(-1),
              _block_diag(w_gate_x[l]).astype(BF16), b_gate_x[l].reshape(-1), lru_lambda[l],
              w_out[l].astype(BF16), ln_ffn2[l].reshape(1, dm), w_ffn2_gate[l].astype(BF16),
              w_ffn2_up[l].astype(BF16), w_ffn2_down[l].astype(BF16))
        last = l == depth - 1
        zero_buf = jnp.zeros((bp, CONV_WIDTH - 1, d_rnn), xp.dtype)
        zero_h = jnp.zeros((bp, d_rnn), xp.dtype)
        xp, kp, vp, hp, cp = _layer(xp, zero_buf, zero_h, None, None, lw, lnf, final_norm=last, **cfg)
        xs, kn, vn, hn, cn = _layer(xs, state_lru_conv[l], state_lru_h[l], cache_k_win[l],
                                    cache_v_win[l], lw, lnf, final_norm=last, **cfg)
        for lst, val in zip(outs, (kp, vp, hp, cp, kn, vn, hn, cn)):
            lst.append(val)
    return (xp, xs) + tuple(jnp.stack(lst) for lst in outs)
```

```python
import functools

import jax
import jax.numpy as jnp
import numpy as np
from jax.experimental import pallas as pl
from jax.experimental.pallas import tpu as pltpu

N_HEADS = 8
HEAD_DIM = 64
D_ATTN = N_HEADS * HEAD_DIM
CONV_WIDTH = 4
LRU_C = 8.0
DILATED = ((128, 1), (512, 4), (2048, 16))
MAX_WINDOW = 2048
BLK = 128
EPS = 1e-6

LANES = 128
SUBLANES = 8
VMEM_LIMIT = 56 * 1024 * 1024

F32 = jnp.float32
BF16 = jnp.bfloat16
NEG_INF = float("-inf")


def _rms(x, g):
    return x * jax.lax.rsqrt(jnp.mean(x * x, axis=-1, keepdims=True) + EPS) * g


def _swiglu(xn, wg_ref, wu_ref, wd_ref):
    gate = jnp.dot(xn, wg_ref[...], preferred_element_type=F32)
    up = jnp.dot(xn, wu_ref[...], preferred_element_type=F32)
    h = (gate * jax.nn.sigmoid(gate) * up).astype(BF16)
    return jnp.dot(h, wd_ref[...], preferred_element_type=F32)


def _resident(shape):
    return pl.BlockSpec(shape, lambda *_: (0,) * len(shape), pipeline_mode=pl.Buffered(1))


def _ffn_in_kernel(x_ref, ln1_ref, wg_ref, wu_ref, wd_ref, lnm_ref, win_ref,
                   x1_ref, q_ref, k_ref, v_ref, u_ref, g_ref):
    x = x_ref[...]
    xn = _rms(x, ln1_ref[...]).astype(BF16)
    x1 = x + 0.5 * _swiglu(xn, wg_ref, wu_ref, wd_ref)
    x1_ref[...] = x1
    zn = _rms(x1, lnm_ref[...]).astype(BF16)
    z = jnp.dot(zn, win_ref[...], preferred_element_type=F32)
    d = D_ATTN
    q_ref[...] = z[:, :d] * (HEAD_DIM ** -0.5)
    k_ref[...] = z[:, d:2 * d]
    v_ref[...] = z[:, 2 * d:3 * d]
    r = u_ref.shape[-1]
    u_ref[...] = z[:, 3 * d:3 * d + r]
    g_ref[...] = z[:, 3 * d + r:]


def _ffn_in(x, ln1, wg, wu, wd, lnm, win, *, tm):
    n, dm = x.shape
    dff = wg.shape[1]
    d_rnn = (win.shape[1] - 3 * D_ATTN) // 2
    row = lambda w: pl.BlockSpec((tm, w), lambda i: (i, 0))
    widths = (dm, D_ATTN, D_ATTN, D_ATTN, d_rnn, d_rnn)
    return pl.pallas_call(
        _ffn_in_kernel,
        grid=(n // tm,),
        in_specs=[row(dm), _resident((1, dm)), _resident((dm, dff)), _resident((dm, dff)),
                  _resident((dff, dm)), _resident((1, dm)), _resident(win.shape)],
        out_specs=[row(w) for w in widths],
        out_shape=[jax.ShapeDtypeStruct((n, w), F32) for w in widths],
        compiler_params=pltpu.CompilerParams(dimension_semantics=("arbitrary",),
                                             vmem_limit_bytes=VMEM_LIMIT),
        name="ffn_in",
    )(x, ln1, wg, wu, wd, lnm, win)


FOLD = 16
BLOCK_UNROLL = 16


def _fold_rows(s):
    return s // FOLD + SUBLANES


def _branch_bias(dil):
    nch = FOLD // dil
    qs = BLK // nch
    ip = np.arange(BLK)
    kp = np.arange(2 * BLK)
    i = nch * (ip % qs) + ip // qs
    kk = nch * (kp % (2 * qs)) + kp // (2 * qs)
    delta = kk[None, :] - i[:, None]
    first = delta <= 0
    other = (delta >= 0) & (delta <= BLK)
    return np.where(np.concatenate([first, other], axis=0), 0.0, NEG_INF).astype(np.float32)


def _attn_prompt_kernel(q_ref, k_ref, v_ref, bias_ref, o_ref, qf, kf, vf, o_acc, m_acc, l_acc):
    s = q_ref.shape[1]
    pr = qf.shape[0] // FOLD
    groups = FOLD // SUBLANES
    nt = (((1,), (1,)), ((), ()))
    low = jax.lax.broadcasted_iota(jnp.int32, (BLK, LANES), 1) < HEAD_DIM

    def fold(mi, c):
        for grp in range(groups):
            src = pl.ds(pl.multiple_of(mi * FOLD + grp * SUBLANES, SUBLANES), SUBLANES)
            dst = pl.ds(mi + grp * SUBLANES * pr, SUBLANES, stride=pr)
            qf[dst, :] = q_ref[0, src, :]
            kf[dst, :] = k_ref[0, src, :]
            vf[dst, :] = v_ref[0, src, :]
        return c

    jax.lax.fori_loop(0, s // FOLD, fold, 0, unroll=8)

    n_branches = len(DILATED)
    for bi, (_, dil) in enumerate(DILATED):
        nch = FOLD // dil
        qs = BLK // nch
        nblk = s // (dil * BLK)
        first, last = bi == 0, bi == n_branches - 1

        def block(idx, c, bi=bi, dil=dil, nch=nch, qs=qs, nblk=nblk, first=first, last=last):
            rd = idx // nblk
            j = idx % nblk
            qbase = rd * pr + qs * j
            kbase = rd * pr + qs * jnp.maximum(j - 1, 0)

            def rows_of(base, a, n):
                return pl.ds(pl.multiple_of(base + a * dil * pr, SUBLANES), n)

            def gather(ref, base, n):
                return jnp.concatenate([ref[rows_of(base, a, n), :] for a in range(nch)], axis=0)

            q2 = gather(qf, qbase, qs).astype(BF16)
            k2 = gather(kf, kbase, 2 * qs).astype(BF16)
            v2 = gather(vf, kbase, 2 * qs).astype(BF16)
            brow = (2 * bi + jnp.minimum(j, 1)) * BLK
            bias = bias_ref[pl.ds(pl.multiple_of(brow, BLK), BLK), :]
            res = []
            for sel in (low, ~low):
                qm = jnp.where(sel, q2, jnp.zeros_like(q2))
                sc = jax.lax.dot_general(qm, k2, nt, preferred_element_type=F32) + bias
                m = jnp.max(sc, axis=-1, keepdims=True)
                p = jnp.exp(sc - m)
                l = jnp.sum(p, axis=-1, keepdims=True)
                o = jnp.dot(p.astype(BF16), v2, preferred_element_type=F32)
                res.append((o, m, l))
            o_b, m_b, l_b = (jnp.where(low, x0, x1) for x0, x1 in zip(*res))
            for a in range(nch):
                rows = rows_of(qbase, a, qs)
                part = slice(a * qs, (a + 1) * qs)
                if first:
                    o_acc[rows, :] = o_b[part]
                    m_acc[rows, :] = m_b[part]
                    l_acc[rows, :] = l_b[part]
                    continue
                m_old = m_acc[rows, :]
                m_new = jnp.maximum(m_old, m_b[part])
                e_old = jnp.exp(m_old - m_new)
                e_b = jnp.exp(m_b[part] - m_new)
                o_new = o_acc[rows, :] * e_old + o_b[part] * e_b
                l_new = l_acc[rows, :] * e_old + l_b[part] * e_b
                if last:
                    o_acc[rows, :] = o_new * (1.0 / l_new)
                else:
                    o_acc[rows, :] = o_new
                    m_acc[rows, :] = m_new
                    l_acc[rows, :] = l_new
            return c

        jax.lax.fori_loop(0, dil * nblk, block, 0, unroll=BLOCK_UNROLL)

    def unfold(mi, c):
        rows = o_acc[pl.ds(mi, FOLD, stride=pr), :]
        o_ref[0, pl.ds(pl.multiple_of(mi * FOLD, FOLD), FOLD), :] = rows.astype(BF16)
        return c

    jax.lax.fori_loop(0, s // FOLD, unfold, 0, unroll=4)


def _attn_prompt(q, k, v):
    b, s, d = q.shape
    assert s % (FOLD * BLK) == 0 and s // FOLD >= 2 * BLK and all(FOLD % dil == 0 for _, dil in DILATED)
    bias = np.concatenate([_branch_bias(dil) for _, dil in DILATED], axis=0)
    seq = pl.BlockSpec((1, s, LANES), lambda bi, hp: (bi, 0, hp), pipeline_mode=pl.Buffered(1))
    buf = pltpu.VMEM((FOLD * _fold_rows(s), LANES), F32)
    return pl.pallas_call(
        _attn_prompt_kernel,
        grid=(b, d // LANES),
        in_specs=[seq, seq, seq, _resident(bias.shape)],
        out_specs=pl.BlockSpec((1, s, LANES), lambda bi, hp: (bi, 0, hp)),
        out_shape=jax.ShapeDtypeStruct((b, s, d), BF16),
        scratch_shapes=[buf] * 6,
        compiler_params=pltpu.CompilerParams(dimension_semantics=("arbitrary", "arbitrary"),
                                             vmem_limit_bytes=VMEM_LIMIT),
        name="attn_prompt",
    )(q, k, v, jnp.asarray(bias))


def _branch_counts(t, w_buf):
    i = np.arange(t)[:, None]
    c = np.arange(w_buf)[None, :]
    n = np.arange(2 * SUBLANES)[None, :]
    cnt_c = np.zeros((t, w_buf), np.float32)
    cnt_n = np.zeros((t, 2 * SUBLANES), np.float32)
    for win, dil in DILATED:
        dist = w_buf + i - c
        cnt_c += ((dist % dil == 0) & (dist >= dil) & (dist <= win)).astype(np.float32)
        dist = i - n
        cnt_n += ((n < t) & (dist >= 0) & (dist % dil == 0) & (dist <= win)).astype(np.float32)
    return np.tile(cnt_c, (N_HEADS, 1)), np.tile(cnt_n, (N_HEADS, 1))


def _attn_sample_kernel(q_ref, kn_ref, vn_ref, kc_ref, vc_ref, cc_ref, cn_ref, o_ref):
    t = q_ref.shape[1]
    q = q_ref[0]
    rows = N_HEADS * t
    q_all = jnp.concatenate([q] * N_HEADS, axis=0)
    row_head = jax.lax.broadcasted_iota(jnp.int32, (rows, D_ATTN), 0) // t
    col_head = jax.lax.broadcasted_iota(jnp.int32, (rows, D_ATTN), 1) // HEAD_DIM
    own = row_head == col_head
    qbd = jnp.where(own, q_all, 0.0).astype(BF16)
    pad = jnp.zeros((2 * SUBLANES - t, D_ATTN), F32)
    kn = jnp.concatenate([kn_ref[0], pad], axis=0).astype(BF16)
    vn = jnp.concatenate([vn_ref[0], pad], axis=0).astype(BF16)
    kc = kc_ref[0].astype(BF16)
    vc = vc_ref[0].astype(BF16)
    nt = (((1,), (1,)), ((), ()))
    s_c = jax.lax.dot_general(qbd, kc, nt, preferred_element_type=F32)
    s_n = jax.lax.dot_general(qbd, kn, nt, preferred_element_type=F32)
    cnt_c = cc_ref[...]
    cnt_n = cn_ref[...]
    s_c = jnp.where(cnt_c > 0.0, s_c, NEG_INF)
    s_n = jnp.where(cnt_n > 0.0, s_n, NEG_INF)
    m = jnp.maximum(jnp.max(s_c, axis=-1, keepdims=True), jnp.max(s_n, axis=-1, keepdims=True))
    p_c = cnt_c * jnp.exp(s_c - m)
    p_n = cnt_n * jnp.exp(s_n - m)
    l = jnp.sum(p_c, axis=-1, keepdims=True) + jnp.sum(p_n, axis=-1, keepdims=True)
    o = (jnp.dot(p_c.astype(BF16), vc, preferred_element_type=F32)
         + jnp.dot(p_n.astype(BF16), vn, preferred_element_type=F32)) * (1.0 / l)
    o = jnp.where(own, o, 0.0)
    acc = o[0:t]
    for h in range(1, N_HEADS):
        acc = acc + o[h * t:(h + 1) * t]
    o_ref[0] = acc


def _attn_sample(q, k_new, v_new, k_cache, v_cache):
    b, t, d = q.shape
    w_buf = k_cache.shape[1]
    cnt_c, cnt_n = _branch_counts(t, w_buf)
    tok = pl.BlockSpec((1, t, d), lambda bi: (bi, 0, 0))
    cache = pl.BlockSpec((1, w_buf, d), lambda bi: (bi, 0, 0))
    return pl.pallas_call(
        _attn_sample_kernel,
        grid=(b,),
        in_specs=[tok, tok, tok, cache, cache, _resident(cnt_c.shape), _resident(cnt_n.shape)],
        out_specs=tok,
        out_shape=jax.ShapeDtypeStruct((b, t, d), F32),
        compiler_params=pltpu.CompilerParams(dimension_semantics=("arbitrary",),
                                             vmem_limit_bytes=VMEM_LIMIT),
        name="attn_sample",
    )(q, k_new, v_new, k_cache, v_cache, jnp.asarray(cnt_c), jnp.asarray(cnt_n))


def _rglru_kernel(u_ref, g_ref, cbuf_ref, h0_ref, cw_ref, cb_ref, wa_ref, ba_ref, wx_ref, bx_ref,
                  lam_ref, r_ref, hlast_ref, tail_ref, ubuf, a_s, b_s, hcar):
    tc = u_ref.shape[1]
    pad = SUBLANES

    @pl.when(pl.program_id(1) == 0)
    def _():
        ubuf[0:pad] = cbuf_ref[0]
        hcar[...] = h0_ref[0]

    ubuf[pad:pad + tc] = u_ref[0]
    first = pad - (CONV_WIDTH - 1)
    xc = cb_ref[...] + cw_ref[0:1, :] * ubuf[first:first + tc]
    for j in range(1, CONV_WIDTH):
        xc = xc + cw_ref[j:j + 1, :] * ubuf[first + j:first + j + tc]
    new_tail = ubuf[tc:tc + pad]
    ubuf[0:pad] = new_tail
    tail_ref[0] = new_tail

    xb = xc.astype(BF16)
    rg = jax.nn.sigmoid(jnp.dot(xb, wa_ref[...], preferred_element_type=F32) + ba_ref[...])
    ig = jax.nn.sigmoid(jnp.dot(xb, wx_ref[...], preferred_element_type=F32) + bx_ref[...])
    log_a = (-LRU_C * jax.nn.softplus(-lam_ref[...])) * rg
    a = jnp.exp(log_a)
    a_s[...] = a
    b_s[...] = jnp.sqrt(-jnp.tanh(log_a) * (a * a + 1.0)) * ig * xc

    row = jax.lax.broadcasted_iota(jnp.int32, (SUBLANES, a_s.shape[1]), 0)

    def group(gi, h):
        r0 = pl.multiple_of(gi * SUBLANES, SUBLANES)
        a = a_s[pl.ds(r0, SUBLANES), :]
        b = b_s[pl.ds(r0, SUBLANES), :]
        for sh in (1, 2, 4):
            a_prev = pltpu.roll(a, sh, axis=0)
            b_prev = pltpu.roll(b, sh, axis=0)
            take = row >= sh
            b = jnp.where(take, a * b_prev + b, b)
            a = jnp.where(take, a * a_prev, a)
        hs = a * h + b
        b_s[pl.ds(r0, SUBLANES), :] = hs
        return hs[SUBLANES - 1:SUBLANES, :]

    h_end = jax.lax.fori_loop(0, tc // SUBLANES, group, hcar[...])
    hcar[...] = h_end
    hlast_ref[0] = h_end
    r_ref[0] = (b_s[...] * jax.nn.gelu(g_ref[0])).astype(BF16)


def _rglru(u, g, conv_buf, h0, conv_w, conv_b, wa_bd, b_a, wx_bd, b_x, lam, *, tc):
    b, t, r = u.shape
    tc = min(tc, t)
    pad = SUBLANES
    cbuf = jnp.pad(conv_buf, ((0, 0), (pad - (CONV_WIDTH - 1), 0), (0, 0)))
    chunk = pl.BlockSpec((1, tc, r), lambda bi, ti: (bi, ti, 0))
    per_seq = lambda rows: pl.BlockSpec((1, rows, r), lambda bi, ti: (bi, 0, 0))
    vec = _resident((1, r))
    out, h_last, tail = pl.pallas_call(
        _rglru_kernel,
        grid=(b, t // tc),
        in_specs=[chunk, chunk, per_seq(pad), per_seq(1), _resident((CONV_WIDTH, r)), vec,
                  _resident((r, r)), vec, _resident((r, r)), vec, vec],
        out_specs=[chunk, per_seq(1), per_seq(pad)],
        out_shape=[jax.ShapeDtypeStruct((b, t, r), BF16),
                   jax.ShapeDtypeStruct((b, 1, r), F32),
                   jax.ShapeDtypeStruct((b, pad, r), F32)],
        scratch_shapes=[pltpu.VMEM((pad + tc, r), F32), pltpu.VMEM((tc, r), F32),
                        pltpu.VMEM((tc, r), F32), pltpu.VMEM((1, r), F32)],
        compiler_params=pltpu.CompilerParams(dimension_semantics=("arbitrary", "arbitrary"),
                                             vmem_limit_bytes=VMEM_LIMIT),
        name="rglru",
    )(u, g, cbuf, h0.reshape(b, 1, r), conv_w, conv_b.reshape(1, r), wa_bd, b_a.reshape(1, r),
      wx_bd, b_x.reshape(1, r), lam.reshape(1, r))
    return out, h_last.reshape(b, r), tail[:, pad - (CONV_WIDTH - 1):]


def _out_ffn_kernel(x1_ref, attn_ref, r_ref, wo_ref, ln2_ref, wg_ref, wu_ref, wd_ref, lnf_ref, y_ref,
                    *, final_norm):
    mix = jnp.concatenate([attn_ref[...].astype(BF16), r_ref[...]], axis=-1)
    x2 = x1_ref[...] + jnp.dot(mix, wo_ref[...], preferred_element_type=F32)
    xn = _rms(x2, ln2_ref[...]).astype(BF16)
    x3 = x2 + 0.5 * _swiglu(xn, wg_ref, wu_ref, wd_ref)
    y_ref[...] = _rms(x3, lnf_ref[...]) if final_norm else x3


def _out_ffn(x1, attn, r, wo, ln2, wg, wu, wd, lnf, *, final_norm, tm):
    n, dm = x1.shape
    dff = wg.shape[1]
    row = lambda w: pl.BlockSpec((tm, w), lambda i: (i, 0))
    return pl.pallas_call(
        functools.partial(_out_ffn_kernel, final_norm=final_norm),
        grid=(n // tm,),
        in_specs=[row(dm), row(D_ATTN), row(r.shape[1]), _resident(wo.shape), _resident((1, dm)),
                  _resident((dm, dff)), _resident((dm, dff)), _resident((dff, dm)), _resident((1, dm))],
        out_specs=row(dm),
        out_shape=jax.ShapeDtypeStruct((n, dm), F32),
        compiler_params=pltpu.CompilerParams(dimension_semantics=("arbitrary",),
                                             vmem_limit_bytes=VMEM_LIMIT),
        name="out_ffn",
    )(x1, attn, r, wo, ln2, wg, wu, wd, lnf)


def _block_diag(w):
    nb, c, d = w.shape
    return jnp.einsum("ncd,nm->ncmd", w, jnp.eye(nb, dtype=w.dtype)).reshape(nb * c, nb * d)


def _layer(x, conv_buf, h0, k_past, v_past, lw, lnf, *, final_norm, tm, tc):
    (ln1, w1g, w1u, w1d, ln_m, w_in, conv_w, conv_b, wa_bd, b_a, wx_bd, b_x, lam,
     w_out, ln2, w2g, w2u, w2d) = lw
    b, t, dm = x.shape
    r = conv_w.shape[-1]
    prompt = k_past is None
    keep = min(MAX_WINDOW, t) if prompt else t
    x1, q, k, v, u, g = _ffn_in(x.reshape(b * t, dm), ln1, w1g, w1u, w1d, ln_m, w_in, tm=tm)
    shp = lambda a: a.reshape(b, t, a.shape[-1])
    if prompt:
        attn = _attn_prompt(shp(q), shp(k), shp(v))
    else:
        attn = _attn_sample(shp(q), shp(k), shp(v),
                            k_past.reshape(b, -1, D_ATTN), v_past.reshape(b, -1, D_ATTN))
    rnn, h_last, new_buf = _rglru(shp(u), shp(g), conv_buf, h0, conv_w, conv_b, wa_bd, b_a, wx_bd,
                                  b_x, lam, tc=tc)
    y = _out_ffn(x1, attn.reshape(b * t, D_ATTN), rnn.reshape(b * t, r), w_out, ln2, w2g, w2u, w2d, lnf,
                 final_norm=final_norm, tm=tm)
    k_state = shp(k)[:, t - keep:].reshape(b, keep, N_HEADS, HEAD_DIM)
    v_state = shp(v)[:, t - keep:].reshape(b, keep, N_HEADS, HEAD_DIM)
    return y.reshape(b, t, dm), k_state, v_state, h_last, new_buf


def kernel(x_prompt, x_sample, cache_k_win, cache_v_win, state_lru_h, state_lru_conv, ln_ffn1, w_ffn1_gate, w_ffn1_up, w_ffn1_down, ln_mix, w_in, conv_w, conv_b, w_gate_a, b_gate_a, w_gate_x, b_gate_x, lru_lambda, w_out, ln_ffn2, w_ffn2_gate, w_ffn2_up, w_ffn2_down, ln_final):
    depth = ln_ffn1.shape[0]
    dm = x_prompt.shape[-1]
    d_rnn = conv_w.shape[-1]
    bp = x_prompt.shape[0]
    xp, xs = x_prompt, x_sample
    outs = [[] for _ in range(8)]
    lnf = ln_final.reshape(1, dm)
    cfg = dict(tm=256, tc=512)
    for l in range(depth):
        lw = (ln_ffn1[l].reshape(1, dm), w_ffn1_gate[l].astype(BF16), w_ffn1_up[l].astype(BF16),
              w_ffn1_down[l].astype(BF16), ln_mix[l].reshape(1, dm), w_in[l].astype(BF16),
              conv_w[l], conv_b[l], _block_diag(w_gate_a[l]).astype(BF16), b_gate_a[l].reshape(-1),
              _block_diag(w_gate_x[l]).astype(BF16), b_gate_x[l].reshape(-1), lru_lambda[l],
              w_out[l].astype(BF16), ln_ffn2[l].reshape(1, dm), w_ffn2_gate[l].astype(BF16),
              w_ffn2_up[l].astype(BF16), w_ffn2_down[l].astype(BF16))
        last = l == depth - 1
        zero_buf = jnp.zeros((bp, CONV_WIDTH - 1, d_rnn), xp.dtype)
        zero_h = jnp.zeros((bp, d_rnn), xp.dtype)
        xp, kp, vp, hp, cp = _layer(xp, zero_buf, zero_h, None, None, lw, lnf, final_norm=last, **cfg)
        xs, kn, vn, hn, cn = _layer(xs, state_lru_conv[l], state_lru_h[l], cache_k_win[l],
                                    cache_v_win[l], lw, lnf, final_norm=last, **cfg)
        for lst, val in zip(outs, (kp, vp, hp, cp, kn, vn, hn, cn)):
            lst.append(val)
    return (xp, xs) + tuple(jnp.stack(lst) for lst in outs)
```

```python
import functools

import jax
import jax.numpy as jnp
import numpy as np
from jax.experimental import pallas as pl
from jax.experimental.pallas import tpu as pltpu

N_HEADS = 8
HEAD_DIM = 64
D_ATTN = N_HEADS * HEAD_DIM
CONV_WIDTH = 4
LRU_C = 8.0
DILATED = ((128, 1), (512, 4), (2048, 16))
MAX_WINDOW = 2048
BLK = 128
EPS = 1e-6

LANES = 128
SUBLANES = 8
VMEM_LIMIT = 56 * 1024 * 1024
VMEM_LIMIT_ATTN = 60 * 1024 * 1024

F32 = jnp.float32
BF16 = jnp.bfloat16
NEG_INF = float("-inf")


def _rms(x, g):
    return x * jax.lax.rsqrt(jnp.mean(x * x, axis=-1, keepdims=True) + EPS) * g


def _swiglu(xn, wg_ref, wu_ref, wd_ref):
    gate = jnp.dot(xn, wg_ref[...], preferred_element_type=F32)
    up = jnp.dot(xn, wu_ref[...], preferred_element_type=F32)
    h = (gate * jax.nn.sigmoid(gate) * up).astype(BF16)
    return jnp.dot(h, wd_ref[...], preferred_element_type=F32)


def _resident(shape):
    return pl.BlockSpec(shape, lambda *_: (0,) * len(shape), pipeline_mode=pl.Buffered(1))


def _ffn_in_kernel(x_ref, ln1_ref, wg_ref, wu_ref, wd_ref, lnm_ref, win_ref,
                   x1_ref, q_ref, k_ref, v_ref, u_ref, g_ref):
    x = x_ref[...]
    xn = _rms(x, ln1_ref[...]).astype(BF16)
    x1 = x + 0.5 * _swiglu(xn, wg_ref, wu_ref, wd_ref)
    x1_ref[...] = x1
    zn = _rms(x1, lnm_ref[...]).astype(BF16)
    z = jnp.dot(zn, win_ref[...], preferred_element_type=F32)
    d = D_ATTN
    q_ref[...] = z[:, :d] * (HEAD_DIM ** -0.5)
    k_ref[...] = z[:, d:2 * d]
    v_ref[...] = z[:, 2 * d:3 * d]
    r = u_ref.shape[-1]
    u_ref[...] = z[:, 3 * d:3 * d + r]
    g_ref[...] = z[:, 3 * d + r:]


def _ffn_in(x, ln1, wg, wu, wd, lnm, win, *, tm):
    n, dm = x.shape
    dff = wg.shape[1]
    d_rnn = (win.shape[1] - 3 * D_ATTN) // 2
    row = lambda w: pl.BlockSpec((tm, w), lambda i: (i, 0))
    widths = (dm, D_ATTN, D_ATTN, D_ATTN, d_rnn, d_rnn)
    return pl.pallas_call(
        _ffn_in_kernel,
        grid=(n // tm,),
        in_specs=[row(dm), _resident((1, dm)), _resident((dm, dff)), _resident((dm, dff)),
                  _resident((dff, dm)), _resident((1, dm)), _resident(win.shape)],
        out_specs=[row(w) for w in widths],
        out_shape=[jax.ShapeDtypeStruct((n, w), F32) for w in widths],
        compiler_params=pltpu.CompilerParams(dimension_semantics=("arbitrary",),
                                             vmem_limit_bytes=VMEM_LIMIT),
        name="ffn_in",
    )(x, ln1, wg, wu, wd, lnm, win)


FOLD = 16
BLOCK_UNROLL = 16


def _fold_rows(s):
    return s // FOLD + SUBLANES


def _branch_bias(dil):
    nch = FOLD // dil
    qs = BLK // nch
    ip = np.arange(BLK)
    kp = np.arange(2 * BLK)
    i = nch * (ip % qs) + ip // qs
    kk = nch * (kp % (2 * qs)) + kp // (2 * qs)
    delta = kk[None, :] - i[:, None]
    first = delta <= 0
    other = (delta >= 0) & (delta <= BLK)
    return np.where(np.concatenate([first, other], axis=0), 0.0, NEG_INF).astype(np.float32)


def _attn_prompt_kernel(q_ref, k_ref, v_ref, bias_ref, o_ref, qf, kf, vf, o_acc, m_acc, l_acc):
    s = q_ref.shape[1]
    pr = qf.shape[0] // FOLD
    groups = FOLD // SUBLANES
    nt = (((1,), (1,)), ((), ()))
    low = jax.lax.broadcasted_iota(jnp.int32, (BLK, LANES), 1) < HEAD_DIM

    def fold(mi, c):
        for grp in range(groups):
            src = pl.ds(pl.multiple_of(mi * FOLD + grp * SUBLANES, SUBLANES), SUBLANES)
            dst = pl.ds(mi + grp * SUBLANES * pr, SUBLANES, stride=pr)
            qf[dst, :] = q_ref[0, src, :]
            kf[dst, :] = k_ref[0, src, :]
            vf[dst, :] = v_ref[0, src, :]
        return c

    jax.lax.fori_loop(0, s // FOLD, fold, 0, unroll=8)

    n_branches = len(DILATED)
    for bi, (_, dil) in enumerate(DILATED):
        nch = FOLD // dil
        qs = BLK // nch
        nblk = s // (dil * BLK)
        first, last = bi == 0, bi == n_branches - 1

        def block(idx, c, bi=bi, dil=dil, nch=nch, qs=qs, nblk=nblk, first=first, last=last):
            rd = idx // nblk
            j = idx % nblk
            qbase = rd * pr + qs * j
            kbase = rd * pr + qs * jnp.maximum(j - 1, 0)

            def rows_of(base, a, n):
                return pl.ds(pl.multiple_of(base + a * dil * pr, SUBLANES), n)

            def gather(ref, base, n):
                return jnp.concatenate([ref[rows_of(base, a, n), :] for a in range(nch)], axis=0)

            q2 = gather(qf, qbase, qs).astype(BF16)
            k2 = gather(kf, kbase, 2 * qs).astype(BF16)
            v2 = gather(vf, kbase, 2 * qs).astype(BF16)
            brow = (2 * bi + jnp.minimum(j, 1)) * BLK
            bias = bias_ref[pl.ds(pl.multiple_of(brow, BLK), BLK), :]
            res = []
            for sel in (low, ~low):
                qm = jnp.where(sel, q2, jnp.zeros_like(q2))
                sc = jax.lax.dot_general(qm, k2, nt, preferred_element_type=F32) + bias
                m = jnp.max(sc, axis=-1, keepdims=True)
                p = jnp.exp(sc - m)
                l = jnp.sum(p, axis=-1, keepdims=True)
                o = jnp.dot(p.astype(BF16), v2, preferred_element_type=F32)
                res.append((o, m, l))
            o_b, m_b, l_b = (jnp.where(low, x0, x1) for x0, x1 in zip(*res))
            for a in range(nch):
                rows = rows_of(qbase, a, qs)
                part = slice(a * qs, (a + 1) * qs)
                if first:
                    o_acc[rows, :] = o_b[part]
                    m_acc[rows, :] = m_b[part]
                    l_acc[rows, :] = l_b[part]
                    continue
                m_old = m_acc[rows, :]
                m_new = jnp.maximum(m_old, m_b[part])
                keep = m_old >= m_b[part]
                e_min = jnp.exp(jnp.minimum(m_old, m_b[part]) - m_new)
                e_old = jnp.where(keep, 1.0, e_min)
                e_b = jnp.where(keep, e_min, 1.0)
                o_new = o_acc[rows, :] * e_old + o_b[part] * e_b
                l_new = l_acc[rows, :] * e_old + l_b[part] * e_b
                if last:
                    o_acc[rows, :] = o_new * (1.0 / l_new)
                else:
                    o_acc[rows, :] = o_new
                    m_acc[rows, :] = m_new
                    l_acc[rows, :] = l_new
            return c

        jax.lax.fori_loop(0, dil * nblk, block, 0, unroll=BLOCK_UNROLL)

    def unfold(mi, c):
        rows = o_acc[pl.ds(mi, FOLD, stride=pr), :]
        o_ref[0, pl.ds(pl.multiple_of(mi * FOLD, FOLD), FOLD), :] = rows.astype(BF16)
        return c

    jax.lax.fori_loop(0, s // FOLD, unfold, 0, unroll=4)


def _attn_prompt(q, k, v):
    b, s, d = q.shape
    assert s % (FOLD * BLK) == 0 and s // FOLD >= 2 * BLK and all(FOLD % dil == 0 for _, dil in DILATED)
    bias = np.concatenate([_branch_bias(dil) for _, dil in DILATED], axis=0)
    seq = pl.BlockSpec((1, s, LANES), lambda bi, hp: (bi, 0, hp))
    buf = pltpu.VMEM((FOLD * _fold_rows(s), LANES), F32)
    return pl.pallas_call(
        _attn_prompt_kernel,
        grid=(b, d // LANES),
        in_specs=[seq, seq, seq, _resident(bias.shape)],
        out_specs=pl.BlockSpec((1, s, LANES), lambda bi, hp: (bi, 0, hp)),
        out_shape=jax.ShapeDtypeStruct((b, s, d), BF16),
        scratch_shapes=[buf] * 6,
        compiler_params=pltpu.CompilerParams(dimension_semantics=("arbitrary", "arbitrary"),
                                             vmem_limit_bytes=VMEM_LIMIT_ATTN),
        name="attn_prompt",
    )(q, k, v, jnp.asarray(bias))


def _branch_counts(t, w_buf):
    i = np.arange(t)[:, None]
    c = np.arange(w_buf)[None, :]
    n = np.arange(LANES)[None, :]
    cnt_c = np.zeros((t, w_buf), np.float32)
    cnt_n = np.zeros((t, LANES), np.float32)
    for win, dil in DILATED:
        dist = w_buf + i - c
        cnt_c += ((dist % dil == 0) & (dist >= dil) & (dist <= win)).astype(np.float32)
        dist = i - n
        cnt_n += ((n < t) & (dist >= 0) & (dist % dil == 0) & (dist <= win)).astype(np.float32)
    return np.concatenate([cnt_c, cnt_n], axis=1)


def _attn_sample_kernel(q_ref, knt_ref, vnt_ref, kc_ref, vc_ref, cnt_ref, ot_ref, knp, vnp, *, t):
    rows = 2 * SUBLANES

    @pl.when(pl.program_id(0) == 0)
    def _():
        knp[...] = jnp.zeros_like(knp)
        vnp[...] = jnp.zeros_like(vnp)

    knp[:, 0:t] = knt_ref[0]
    vnp[:, 0:t] = vnt_ref[0]
    cnt = cnt_ref[...]
    valid = cnt > 0.0
    q = jnp.concatenate([q_ref[0], jnp.zeros((rows - t, D_ATTN), F32)], axis=0).astype(BF16)
    ones = jnp.ones((SUBLANES, cnt.shape[1]), F32)
    nt = (((1,), (1,)), ((), ()))
    for h in range(N_HEADS):
        hrows = slice(h * HEAD_DIM, (h + 1) * HEAD_DIM)
        kt = jnp.concatenate([kc_ref[0, h], knp[hrows, :]], axis=1).astype(BF16)
        sc = jnp.dot(q[:, hrows], kt, preferred_element_type=F32)
        sc = jnp.where(valid, sc, NEG_INF)
        m = jnp.maximum(jnp.max(sc, axis=-1, keepdims=True), -1e30)
        p = (cnt * jnp.exp(sc - m)).astype(BF16)
        vt = jnp.concatenate([vc_ref[0, h], vnp[hrows, :]], axis=1)
        vt = jnp.concatenate([vt, ones], axis=0).astype(BF16)
        o = jax.lax.dot_general(vt, p, nt, preferred_element_type=F32)
        ot_ref[0, hrows, :] = (o[0:HEAD_DIM] * (1.0 / o[HEAD_DIM:HEAD_DIM + 1]))[:, 0:t]


def _attn_sample(q, k_new, v_new, k_cache_t, v_cache_t):
    b, t, d = q.shape
    w = k_cache_t.shape[3]
    assert t <= SUBLANES and w % LANES == 0
    cnt = np.zeros((2 * SUBLANES, w + LANES), np.float32)
    cnt[:t] = _branch_counts(t, w)
    tr = lambda a: jnp.swapaxes(a, 1, 2)
    tok = pl.BlockSpec((1, t, d), lambda bi: (bi, 0, 0))
    tok_t = pl.BlockSpec((1, d, t), lambda bi: (bi, 0, 0))
    cache = pl.BlockSpec((1,) + k_cache_t.shape[1:], lambda bi: (bi, 0, 0, 0))
    out = pl.pallas_call(
        functools.partial(_attn_sample_kernel, t=t),
        grid=(b,),
        in_specs=[tok, tok_t, tok_t, cache, cache, _resident(cnt.shape)],
        out_specs=tok_t,
        out_shape=jax.ShapeDtypeStruct((b, d, t), F32),
        scratch_shapes=[pltpu.VMEM((d, LANES), F32), pltpu.VMEM((d, LANES), F32)],
        compiler_params=pltpu.CompilerParams(dimension_semantics=("arbitrary",),
                                             vmem_limit_bytes=VMEM_LIMIT),
        name="attn_sample",
    )(q, tr(k_new), tr(v_new), k_cache_t, v_cache_t, jnp.asarray(cnt))
    return tr(out)


def _rglru_kernel(u_ref, g_ref, cbuf_ref, h0_ref, cw_ref, cb_ref, wa_ref, ba_ref, wx_ref, bx_ref,
                  lam_ref, r_ref, hlast_ref, tail_ref, ubuf, a_s, b_s, hcar):
    tc = u_ref.shape[1]
    pad = SUBLANES

    @pl.when(pl.program_id(1) == 0)
    def _():
        ubuf[0:pad] = cbuf_ref[0]
        hcar[...] = h0_ref[0]

    ubuf[pad:pad + tc] = u_ref[0]
    first = pad - (CONV_WIDTH - 1)
    xc = cb_ref[...] + cw_ref[0:1, :] * ubuf[first:first + tc]
    for j in range(1, CONV_WIDTH):
        xc = xc + cw_ref[j:j + 1, :] * ubuf[first + j:first + j + tc]
    new_tail = ubuf[tc:tc + pad]
    ubuf[0:pad] = new_tail
    tail_ref[0] = new_tail

    xb = xc.astype(BF16)
    rg = jax.nn.sigmoid(jnp.dot(xb, wa_ref[...], preferred_element_type=F32) + ba_ref[...])
    ig = jax.nn.sigmoid(jnp.dot(xb, wx_ref[...], preferred_element_type=F32) + bx_ref[...])
    log_a = (-LRU_C * jax.nn.softplus(-lam_ref[...])) * rg
    a = jnp.exp(log_a)
    a_s[...] = a
    b_s[...] = jnp.sqrt(-jnp.tanh(log_a) * (a * a + 1.0)) * ig * xc

    row = jax.lax.broadcasted_iota(jnp.int32, (SUBLANES, a_s.shape[1]), 0)

    def group(gi, h):
        r0 = pl.multiple_of(gi * SUBLANES, SUBLANES)
        a = a_s[pl.ds(r0, SUBLANES), :]
        b = b_s[pl.ds(r0, SUBLANES), :]
        for sh in (1, 2, 4):
            a_prev = pltpu.roll(a, sh, axis=0)
            b_prev = pltpu.roll(b, sh, axis=0)
            take = row >= sh
            b = jnp.where(take, a * b_prev + b, b)
            a = jnp.where(take, a * a_prev, a)
        hs = a * h + b
        b_s[pl.ds(r0, SUBLANES), :] = hs
        return hs[SUBLANES - 1:SUBLANES, :]

    h_end = jax.lax.fori_loop(0, tc // SUBLANES, group, hcar[...])
    hcar[...] = h_end
    hlast_ref[0] = h_end
    r_ref[0] = (b_s[...] * jax.nn.gelu(g_ref[0])).astype(BF16)


def _rglru(u, g, conv_buf, h0, conv_w, conv_b, wa_bd, b_a, wx_bd, b_x, lam, *, tc):
    b, t, r = u.shape
    tc = min(tc, t)
    pad = SUBLANES
    cbuf = jnp.pad(conv_buf, ((0, 0), (pad - (CONV_WIDTH - 1), 0), (0, 0)))
    chunk = pl.BlockSpec((1, tc, r), lambda bi, ti: (bi, ti, 0))
    per_seq = lambda rows: pl.BlockSpec((1, rows, r), lambda bi, ti: (bi, 0, 0))
    vec = _resident((1, r))
    out, h_last, tail = pl.pallas_call(
        _rglru_kernel,
        grid=(b, t // tc),
        in_specs=[chunk, chunk, per_seq(pad), per_seq(1), _resident((CONV_WIDTH, r)), vec,
                  _resident((r, r)), vec, _resident((r, r)), vec, vec],
        out_specs=[chunk, per_seq(1), per_seq(pad)],
        out_shape=[jax.ShapeDtypeStruct((b, t, r), BF16),
                   jax.ShapeDtypeStruct((b, 1, r), F32),
                   jax.ShapeDtypeStruct((b, pad, r), F32)],
        scratch_shapes=[pltpu.VMEM((pad + tc, r), F32), pltpu.VMEM((tc, r), F32),
                        pltpu.VMEM((tc, r), F32), pltpu.VMEM((1, r), F32)],
        compiler_params=pltpu.CompilerParams(dimension_semantics=("arbitrary", "arbitrary"),
                                             vmem_limit_bytes=VMEM_LIMIT),
        name="rglru",
    )(u, g, cbuf, h0.reshape(b, 1, r), conv_w, conv_b.reshape(1, r), wa_bd, b_a.reshape(1, r),
      wx_bd, b_x.reshape(1, r), lam.reshape(1, r))
    return out, h_last.reshape(b, r), tail[:, pad - (CONV_WIDTH - 1):]


def _out_ffn_kernel(x1_ref, attn_ref, r_ref, wo_ref, ln2_ref, wg_ref, wu_ref, wd_ref, lnf_ref, y_ref,
                    *, final_norm):
    mix = jnp.concatenate([attn_ref[...].astype(BF16), r_ref[...]], axis=-1)
    x2 = x1_ref[...] + jnp.dot(mix, wo_ref[...], preferred_element_type=F32)
    xn = _rms(x2, ln2_ref[...]).astype(BF16)
    x3 = x2 + 0.5 * _swiglu(xn, wg_ref, wu_ref, wd_ref)
    y_ref[...] = _rms(x3, lnf_ref[...]) if final_norm else x3


def _out_ffn(x1, attn, r, wo, ln2, wg, wu, wd, lnf, *, final_norm, tm):
    n, dm = x1.shape
    dff = wg.shape[1]
    row = lambda w: pl.BlockSpec((tm, w), lambda i: (i, 0))
    return pl.pallas_call(
        functools.partial(_out_ffn_kernel, final_norm=final_norm),
        grid=(n // tm,),
        in_specs=[row(dm), row(D_ATTN), row(r.shape[1]), _resident(wo.shape), _resident((1, dm)),
                  _resident((dm, dff)), _resident((dm, dff)), _resident((dff, dm)), _resident((1, dm))],
        out_specs=row(dm),
        out_shape=jax.ShapeDtypeStruct((n, dm), F32),
        compiler_params=pltpu.CompilerParams(dimension_semantics=("arbitrary",),
                                             vmem_limit_bytes=VMEM_LIMIT),
        name="out_ffn",
    )(x1, attn, r, wo, ln2, wg, wu, wd, lnf)


def _block_diag(w):
    nb, c, d = w.shape
    return jnp.einsum("ncd,nm->ncmd", w, jnp.eye(nb, dtype=w.dtype)).reshape(nb * c, nb * d)


def _layer(x, conv_buf, h0, k_past, v_past, lw, lnf, *, final_norm, tm, tc):
    (ln1, w1g, w1u, w1d, ln_m, w_in, conv_w, conv_b, wa_bd, b_a, wx_bd, b_x, lam,
     w_out, ln2, w2g, w2u, w2d) = lw
    b, t, dm = x.shape
    r = conv_w.shape[-1]
    prompt = k_past is None
    keep = min(MAX_WINDOW, t) if prompt else t
    x1, q, k, v, u, g = _ffn_in(x.reshape(b * t, dm), ln1, w1g, w1u, w1d, ln_m, w_in, tm=tm)
    shp = lambda a: a.reshape(b, t, a.shape[-1])
    if prompt:
        attn = _attn_prompt(shp(q), shp(k), shp(v))
    else:
        attn = _attn_sample(shp(q), shp(k), shp(v),
                            jnp.transpose(k_past, (0, 2, 3, 1)), jnp.transpose(v_past, (0, 2, 3, 1)))
    rnn, h_last, new_buf = _rglru(shp(u), shp(g), conv_buf, h0, conv_w, conv_b, wa_bd, b_a, wx_bd,
                                  b_x, lam, tc=tc)
    y = _out_ffn(x1, attn.reshape(b * t, D_ATTN), rnn.reshape(b * t, r), w_out, ln2, w2g, w2u, w2d, lnf,
                 final_norm=final_norm, tm=tm)
    k_state = shp(k)[:, t - keep:].reshape(b, keep, N_HEADS, HEAD_DIM)
    v_state = shp(v)[:, t - keep:].reshape(b, keep, N_HEADS, HEAD_DIM)
    return y.reshape(b, t, dm), k_state, v_state, h_last, new_buf


def kernel(x_prompt, x_sample, cache_k_win, cache_v_win, state_lru_h, state_lru_conv, ln_ffn1, w_ffn1_gate, w_ffn1_up, w_ffn1_down, ln_mix, w_in, conv_w, conv_b, w_gate_a, b_gate_a, w_gate_x, b_gate_x, lru_lambda, w_out, ln_ffn2, w_ffn2_gate, w_ffn2_up, w_ffn2_down, ln_final):
    depth = ln_ffn1.shape[0]
    dm = x_prompt.shape[-1]
    d_rnn = conv_w.shape[-1]
    bp = x_prompt.shape[0]
    xp, xs = x_prompt, x_sample
    outs = [[] for _ in range(8)]
    lnf = ln_final.reshape(1, dm)
    cfg = dict(tm=256, tc=512)
    for l in range(depth):
        lw = (ln_ffn1[l].reshape(1, dm), w_ffn1_gate[l].astype(BF16), w_ffn1_up[l].astype(BF16),
              w_ffn1_down[l].astype(BF16), ln_mix[l].reshape(1, dm), w_in[l].astype(BF16),
              conv_w[l], conv_b[l], _block_diag(w_gate_a[l]).astype(BF16), b_gate_a[l].reshape(-1),
              _block_diag(w_gate_x[l]).astype(BF16), b_gate_x[l].reshape(-1), lru_lambda[l],
              w_out[l].astype(BF16), ln_ffn2[l].reshape(1, dm), w_ffn2_gate[l].astype(BF16),
              w_ffn2_up[l].astype(BF16), w_ffn2_down[l].astype(BF16))
        last = l == depth - 1
        zero_buf = jnp.zeros((bp, CONV_WIDTH - 1, d_rnn), xp.dtype)
        zero_h = jnp.zeros((bp, d_rnn), xp.dtype)
        xp, kp, vp, hp, cp = _layer(xp, zero_buf, zero_h, None, None, lw, lnf, final_norm=last, **cfg)
        xs, kn, vn, hn, cn = _layer(xs, state_lru_conv[l], state_lru_h[l], cache_k_win[l],
                                    cache_v_win[l], lw, lnf, final_norm=last, **cfg)
        for lst, val in zip(outs, (kp, vp, hp, cp, kn, vn, hn, cn)):
            lst.append(val)
    return (xp, xs) + tuple(jnp.stack(lst) for lst in outs)
```

```python
import functools

import jax
import jax.numpy as jnp
import numpy as np
from jax.experimental import pallas as pl
from jax.experimental.pallas import tpu as pltpu

N_HEADS = 8
HEAD_DIM = 64
D_ATTN = N_HEADS * HEAD_DIM
CONV_WIDTH = 4
LRU_C = 8.0
DILATED = ((128, 1), (512, 4), (2048, 16))
MAX_WINDOW = 2048
BLK = 128
EPS = 1e-6

LANES = 128
SUBLANES = 8
VMEM_LIMIT = 56 * 1024 * 1024
VMEM_LIMIT_ATTN = 60 * 1024 * 1024

F32 = jnp.float32
BF16 = jnp.bfloat16
NEG_INF = float("-inf")


def _rms(x, g):
    return x * jax.lax.rsqrt(jnp.mean(x * x, axis=-1, keepdims=True) + EPS) * g


def _swiglu(xn, wg_ref, wu_ref, wd_ref):
    gate = jnp.dot(xn, wg_ref[...], preferred_element_type=F32)
    up = jnp.dot(xn, wu_ref[...], preferred_element_type=F32)
    h = (gate * jax.nn.sigmoid(gate) * up).astype(BF16)
    return jnp.dot(h, wd_ref[...], preferred_element_type=F32)


def _resident(shape):
    return pl.BlockSpec(shape, lambda *_: (0,) * len(shape), pipeline_mode=pl.Buffered(1))


def _ffn_in_body(x_ref, ln1_ref, wg_ref, wu_ref, wd_ref, lnm_ref, win_ref, x1_ref, q_ref, k_ref, v_ref):
    x = x_ref[...]
    xn = _rms(x, ln1_ref[...]).astype(BF16)
    x1 = x + 0.5 * _swiglu(xn, wg_ref, wu_ref, wd_ref)
    x1_ref[...] = x1
    zn = _rms(x1, lnm_ref[...]).astype(BF16)
    z = jnp.dot(zn, win_ref[...], preferred_element_type=F32)
    d = D_ATTN
    q_ref[...] = z[:, :d] * (HEAD_DIM ** -0.5)
    k_ref[...] = z[:, d:2 * d]
    v_ref[...] = z[:, 2 * d:3 * d]
    r = (z.shape[1] - 3 * d) // 2
    return z[:, 3 * d:3 * d + r], z[:, 3 * d + r:]


def _ffn_in_kernel(x_ref, ln1_ref, wg_ref, wu_ref, wd_ref, lnm_ref, win_ref,
                   x1_ref, q_ref, k_ref, v_ref, u_ref, g_ref):
    u, g = _ffn_in_body(x_ref, ln1_ref, wg_ref, wu_ref, wd_ref, lnm_ref, win_ref, x1_ref, q_ref, k_ref, v_ref)
    u_ref[...] = u
    g_ref[...] = g


def _ffn_in_rglru_kernel(x_ref, ln1_ref, wg_ref, wu_ref, wd_ref, lnm_ref, win_ref,
                         cbuf_ref, h0_ref, cw_ref, cb_ref, wa_ref, ba_ref, wx_ref, bx_ref, lam_ref,
                         x1_ref, q_ref, k_ref, v_ref, kt_ref, vt_ref, r_ref, hlast_ref, tail_ref,
                         u_s, g_s, ubuf, tail_s, hcar, *, tiles_per_seq):
    i = pl.program_id(0)

    @pl.when(i == 0)
    def _():
        for ref in (u_s, g_s, tail_s, hcar):
            ref[...] = jnp.zeros_like(ref)

    starts = (i - 1) % tiles_per_seq == 0
    tail = jnp.where(starts, cbuf_ref[0], tail_s[...])
    h_prev = jnp.where(starts, h0_ref[0], hcar[...])
    r, new_tail, h = _rglru_tile(u_s[...], g_s[...], tail, h_prev, ubuf, cw_ref, cb_ref, wa_ref, ba_ref,
                                 wx_ref, bx_ref, lam_ref)
    u, g = _ffn_in_body(x_ref, ln1_ref, wg_ref, wu_ref, wd_ref, lnm_ref, win_ref, x1_ref, q_ref, k_ref, v_ref)
    kt_ref[0] = k_ref[...].T
    vt_ref[0] = v_ref[...].T
    tail_s[...] = new_tail
    hcar[...] = h
    r_ref[...] = r
    hlast_ref[0] = h
    tail_ref[0] = new_tail
    u_s[...] = u
    g_s[...] = g


def _ffn_in(x, ln1, wg, wu, wd, lnm, win, *, tm):
    n, dm = x.shape
    dff = wg.shape[1]
    d_rnn = (win.shape[1] - 3 * D_ATTN) // 2
    row = lambda w: pl.BlockSpec((tm, w), lambda i: (i, 0))
    widths = (dm, D_ATTN, D_ATTN, D_ATTN, d_rnn, d_rnn)
    return pl.pallas_call(
        _ffn_in_kernel,
        grid=(n // tm,),
        in_specs=[row(dm), _resident((1, dm)), _resident((dm, dff)), _resident((dm, dff)),
                  _resident((dff, dm)), _resident((1, dm)), _resident(win.shape)],
        out_specs=[row(w) for w in widths],
        out_shape=[jax.ShapeDtypeStruct((n, w), F32) for w in widths],
        compiler_params=pltpu.CompilerParams(dimension_semantics=("arbitrary",),
                                             vmem_limit_bytes=VMEM_LIMIT),
        name="ffn_in",
    )(x, ln1, wg, wu, wd, lnm, win)


def _ffn_in_rglru(x, ln1, wg, wu, wd, lnm, win, conv_buf, h0, conv_w, conv_b, wa_bd, b_a, wx_bd, b_x, lam,
                  *, seq, keep, tm):
    n, dm = x.shape
    dff = wg.shape[1]
    r = conv_w.shape[-1]
    nb = n // seq
    tiles = n // tm
    tps = seq // tm
    first_kept = tps - keep // tm
    pad = SUBLANES
    cbuf = jnp.pad(conv_buf, ((0, 0), (pad - (CONV_WIDTH - 1), 0), (0, 0)))
    cur_tile = lambda i: jnp.minimum(i, tiles - 1)
    cur = lambda w: pl.BlockSpec((tm, w), lambda i: (cur_tile(i), 0))
    kept = pl.BlockSpec((1, D_ATTN, tm), lambda i: (cur_tile(i) // tps, 0,
                                                    jnp.maximum(cur_tile(i) % tps - first_kept, 0)))
    prev = lambda w: pl.BlockSpec((tm, w), lambda i: (jnp.maximum(i - 1, 0), 0))
    prev_seq = lambda rows: pl.BlockSpec((1, rows, r), lambda i: (jnp.maximum(i - 1, 0) // tps, 0, 0))
    vec = _resident((1, r))
    x1, q, k, v, kt, vt, rnn, h_last, tail = pl.pallas_call(
        functools.partial(_ffn_in_rglru_kernel, tiles_per_seq=tps),
        grid=(tiles + 1,),
        in_specs=[cur(dm), _resident((1, dm)), _resident((dm, dff)), _resident((dm, dff)),
                  _resident((dff, dm)), _resident((1, dm)), _resident(win.shape),
                  prev_seq(pad), prev_seq(1), _resident((CONV_WIDTH, r)), vec,
                  _resident((r, r)), vec, _resident((r, r)), vec, vec],
        out_specs=[cur(dm), cur(D_ATTN), cur(D_ATTN), cur(D_ATTN), kept, kept,
                   prev(r), prev_seq(1), prev_seq(pad)],
        out_shape=[jax.ShapeDtypeStruct((n, dm), F32)] + [jax.ShapeDtypeStruct((n, D_ATTN), F32)] * 3
                  + [jax.ShapeDtypeStruct((nb, D_ATTN, keep), F32)] * 2
                  + [jax.ShapeDtypeStruct((n, r), BF16), jax.ShapeDtypeStruct((nb, 1, r), F32),
                     jax.ShapeDtypeStruct((nb, pad, r), F32)],
        scratch_shapes=[pltpu.VMEM((tm, r), F32), pltpu.VMEM((tm, r), F32), pltpu.VMEM((pad + tm, r), F32),
                        pltpu.VMEM((pad, r), F32), pltpu.VMEM((1, r), F32)],
        compiler_params=pltpu.CompilerParams(dimension_semantics=("arbitrary",),
                                             vmem_limit_bytes=VMEM_LIMIT),
        name="ffn_in_rglru",
    )(x, ln1, wg, wu, wd, lnm, win, cbuf, h0.reshape(nb, 1, r), conv_w, conv_b.reshape(1, r),
      wa_bd, b_a.reshape(1, r), wx_bd, b_x.reshape(1, r), lam.reshape(1, r))
    return x1, q, k, v, kt, vt, rnn, h_last.reshape(nb, r), tail[:, pad - (CONV_WIDTH - 1):]


FOLD = 16
BLOCK_UNROLL = 16


def _fold_rows(s):
    return s // FOLD + SUBLANES


def _branch_bias(dil):
    nch = FOLD // dil
    qs = BLK // nch
    ip = np.arange(BLK)
    kp = np.arange(2 * BLK)
    i = nch * (ip % qs) + ip // qs
    kk = nch * (kp % (2 * qs)) + kp // (2 * qs)
    delta = kk[None, :] - i[:, None]
    first = delta <= 0
    other = (delta >= 0) & (delta <= BLK)
    return np.where(np.concatenate([first, other], axis=0), 0.0, NEG_INF).astype(np.float32)


def _attn_prompt_kernel(q_ref, k_ref, v_ref, bias_ref, o_ref, qf, kf, vf, o_acc, m_acc, l_acc):
    s = q_ref.shape[1]
    pr = qf.shape[0] // FOLD
    groups = FOLD // SUBLANES
    nt = (((1,), (1,)), ((), ()))
    low = jax.lax.broadcasted_iota(jnp.int32, (BLK, LANES), 1) < HEAD_DIM

    def fold(mi, c):
        for grp in range(groups):
            src = pl.ds(pl.multiple_of(mi * FOLD + grp * SUBLANES, SUBLANES), SUBLANES)
            dst = pl.ds(mi + grp * SUBLANES * pr, SUBLANES, stride=pr)
            qf[dst, :] = q_ref[0, src, :]
            kf[dst, :] = k_ref[0, src, :]
            vf[dst, :] = v_ref[0, src, :]
        return c

    jax.lax.fori_loop(0, s // FOLD, fold, 0, unroll=8)

    n_branches = len(DILATED)
    for bi, (_, dil) in enumerate(DILATED):
        nch = FOLD // dil
        qs = BLK // nch
        nblk = s // (dil * BLK)
        first, last = bi == 0, bi == n_branches - 1

        def block(idx, c, bi=bi, dil=dil, nch=nch, qs=qs, nblk=nblk, first=first, last=last):
            rd = idx // nblk
            j = idx % nblk
            qbase = rd * pr + qs * j
            kbase = rd * pr + qs * jnp.maximum(j - 1, 0)

            def rows_of(base, a, n):
                return pl.ds(pl.multiple_of(base + a * dil * pr, SUBLANES), n)

            def gather(ref, base, n):
                return jnp.concatenate([ref[rows_of(base, a, n), :] for a in range(nch)], axis=0)

            q2 = gather(qf, qbase, qs).astype(BF16)
            k2 = gather(kf, kbase, 2 * qs).astype(BF16)
            v2 = gather(vf, kbase, 2 * qs).astype(BF16)
            brow = (2 * bi + jnp.minimum(j, 1)) * BLK
            bias = bias_ref[pl.ds(pl.multiple_of(brow, BLK), BLK), :]
            res = []
            for sel in (low, ~low):
                qm = jnp.where(sel, q2, jnp.zeros_like(q2))
                sc = jax.lax.dot_general(qm, k2, nt, preferred_element_type=F32) + bias
                m = jnp.max(sc, axis=-1, keepdims=True)
                p = jnp.exp(sc - m)
                l = jnp.sum(p, axis=-1, keepdims=True)
                o = jnp.dot(p.astype(BF16), v2, preferred_element_type=F32)
                res.append((o, m, l))
            o_b, m_b, l_b = (jnp.where(low, x0, x1) for x0, x1 in zip(*res))
            for a in range(nch):
                rows = rows_of(qbase, a, qs)
                part = slice(a * qs, (a + 1) * qs)
                if first:
                    o_acc[rows, :] = o_b[part]
                    m_acc[rows, :] = m_b[part]
                    l_acc[rows, :] = l_b[part]
                    continue
                m_old = m_acc[rows, :]
                m_new = jnp.maximum(m_old, m_b[part])
                keep = m_old >= m_b[part]
                e_min = jnp.exp(jnp.minimum(m_old, m_b[part]) - m_new)
                e_old = jnp.where(keep, 1.0, e_min)
                e_b = jnp.where(keep, e_min, 1.0)
                o_new = o_acc[rows, :] * e_old + o_b[part] * e_b
                l_new = l_acc[rows, :] * e_old + l_b[part] * e_b
                if last:
                    o_acc[rows, :] = o_new * (1.0 / l_new)
                else:
                    o_acc[rows, :] = o_new
                    m_acc[rows, :] = m_new
                    l_acc[rows, :] = l_new
            return c

        jax.lax.fori_loop(0, dil * nblk, block, 0, unroll=BLOCK_UNROLL)

    def unfold(mi, c):
        rows = o_acc[pl.ds(mi, FOLD, stride=pr), :]
        o_ref[0, pl.ds(pl.multiple_of(mi * FOLD, FOLD), FOLD), :] = rows.astype(BF16)
        return c

    jax.lax.fori_loop(0, s // FOLD, unfold, 0, unroll=4)


def _attn_prompt(q, k, v):
    b, s, d = q.shape
    assert s % (FOLD * BLK) == 0 and s // FOLD >= 2 * BLK and all(FOLD % dil == 0 for _, dil in DILATED)
    bias = np.concatenate([_branch_bias(dil) for _, dil in DILATED], axis=0)
    seq = pl.BlockSpec((1, s, LANES), lambda bi, hp: (bi, 0, hp))
    buf = pltpu.VMEM((FOLD * _fold_rows(s), LANES), F32)
    return pl.pallas_call(
        _attn_prompt_kernel,
        grid=(b, d // LANES),
        in_specs=[seq, seq, seq, _resident(bias.shape)],
        out_specs=pl.BlockSpec((1, s, LANES), lambda bi, hp: (bi, 0, hp)),
        out_shape=jax.ShapeDtypeStruct((b, s, d), BF16),
        scratch_shapes=[buf] * 6,
        compiler_params=pltpu.CompilerParams(dimension_semantics=("arbitrary", "arbitrary"),
                                             vmem_limit_bytes=VMEM_LIMIT_ATTN),
        name="attn_prompt",
    )(q, k, v, jnp.asarray(bias))


def _branch_counts(t, w_buf):
    i = np.arange(t)[:, None]
    c = np.arange(w_buf)[None, :]
    n = np.arange(LANES)[None, :]
    cnt_c = np.zeros((t, w_buf), np.float32)
    cnt_n = np.zeros((t, LANES), np.float32)
    for win, dil in DILATED:
        dist = w_buf + i - c
        cnt_c += ((dist % dil == 0) & (dist >= dil) & (dist <= win)).astype(np.float32)
        dist = i - n
        cnt_n += ((n < t) & (dist >= 0) & (dist % dil == 0) & (dist <= win)).astype(np.float32)
    return np.concatenate([cnt_c, cnt_n], axis=1)


def _attn_sample_kernel(q_ref, knt_ref, vnt_ref, kc_ref, vc_ref, cnt_ref, ot_ref, knp, vnp, *, t):
    rows = 2 * SUBLANES

    @pl.when(pl.program_id(0) == 0)
    def _():
        knp[...] = jnp.zeros_like(knp)
        vnp[...] = jnp.zeros_like(vnp)

    knp[:, 0:t] = knt_ref[0]
    vnp[:, 0:t] = vnt_ref[0]
    cnt = cnt_ref[...]
    valid = cnt > 0.0
    q = jnp.concatenate([q_ref[0], jnp.zeros((rows - t, D_ATTN), F32)], axis=0).astype(BF16)
    ones = jnp.ones((SUBLANES, cnt.shape[1]), F32)
    nt = (((1,), (1,)), ((), ()))
    for h in range(N_HEADS):
        hrows = slice(h * HEAD_DIM, (h + 1) * HEAD_DIM)
        kt = jnp.concatenate([kc_ref[0, h], knp[hrows, :]], axis=1).astype(BF16)
        sc = jnp.dot(q[:, hrows], kt, preferred_element_type=F32)
        sc = jnp.where(valid, sc, NEG_INF)
        m = jnp.maximum(jnp.max(sc, axis=-1, keepdims=True), -1e30)
        p = (cnt * jnp.exp(sc - m)).astype(BF16)
        vt = jnp.concatenate([vc_ref[0, h], vnp[hrows, :]], axis=1)
        vt = jnp.concatenate([vt, ones], axis=0).astype(BF16)
        o = jax.lax.dot_general(vt, p, nt, preferred_element_type=F32)
        ot_ref[0, hrows, :] = (o[0:HEAD_DIM] * (1.0 / o[HEAD_DIM:HEAD_DIM + 1]))[:, 0:t]


def _attn_sample(q, k_new, v_new, k_cache_t, v_cache_t):
    b, t, d = q.shape
    w = k_cache_t.shape[3]
    assert t <= SUBLANES and w % LANES == 0
    cnt = np.zeros((2 * SUBLANES, w + LANES), np.float32)
    cnt[:t] = _branch_counts(t, w)
    tr = lambda a: jnp.swapaxes(a, 1, 2)
    tok = pl.BlockSpec((1, t, d), lambda bi: (bi, 0, 0))
    tok_t = pl.BlockSpec((1, d, t), lambda bi: (bi, 0, 0))
    cache = pl.BlockSpec((1,) + k_cache_t.shape[1:], lambda bi: (bi, 0, 0, 0))
    out = pl.pallas_call(
        functools.partial(_attn_sample_kernel, t=t),
        grid=(b,),
        in_specs=[tok, tok_t, tok_t, cache, cache, _resident(cnt.shape)],
        out_specs=tok_t,
        out_shape=jax.ShapeDtypeStruct((b, d, t), F32),
        scratch_shapes=[pltpu.VMEM((d, LANES), F32), pltpu.VMEM((d, LANES), F32)],
        compiler_params=pltpu.CompilerParams(dimension_semantics=("arbitrary",),
                                             vmem_limit_bytes=VMEM_LIMIT),
        name="attn_sample",
    )(q, tr(k_new), tr(v_new), k_cache_t, v_cache_t, jnp.asarray(cnt))
    return tr(out)


def _rglru_tile(u, g, tail, h_prev, ubuf, cw_ref, cb_ref, wa_ref, ba_ref, wx_ref, bx_ref, lam_ref):
    tc = u.shape[0]
    pad = SUBLANES
    ubuf[0:pad] = tail
    ubuf[pad:pad + tc] = u
    first = pad - (CONV_WIDTH - 1)
    xc = cb_ref[...] + cw_ref[0:1, :] * ubuf[first:first + tc]
    for j in range(1, CONV_WIDTH):
        xc = xc + cw_ref[j:j + 1, :] * ubuf[first + j:first + j + tc]
    new_tail = ubuf[tc:tc + pad]

    xb = xc.astype(BF16)
    rg = jax.nn.sigmoid(jnp.dot(xb, wa_ref[...], preferred_element_type=F32) + ba_ref[...])
    ig = jax.nn.sigmoid(jnp.dot(xb, wx_ref[...], preferred_element_type=F32) + bx_ref[...])
    log_a = (-LRU_C * jax.nn.softplus(-lam_ref[...])) * rg
    a = jnp.exp(log_a)
    b = jnp.sqrt(-jnp.tanh(log_a) * (a * a + 1.0)) * ig * xc

    row = jax.lax.broadcasted_iota(jnp.int32, (SUBLANES, u.shape[1]), 0)
    h = h_prev
    hs = []
    for gi in range(tc // SUBLANES):
        ag = a[gi * SUBLANES:(gi + 1) * SUBLANES]
        bg = b[gi * SUBLANES:(gi + 1) * SUBLANES]
        for sh in (1, 2, 4):
            a_prev = pltpu.roll(ag, sh, axis=0)
            b_prev = pltpu.roll(bg, sh, axis=0)
            take = row >= sh
            bg = jnp.where(take, ag * b_prev + bg, bg)
            ag = jnp.where(take, ag * a_prev, ag)
        hg = ag * h + bg
        hs.append(hg)
        h = hg[SUBLANES - 1:SUBLANES, :]
    r = (jnp.concatenate(hs, axis=0) * jax.nn.gelu(g)).astype(BF16)
    return r, new_tail, h


def _rglru_kernel(u_ref, g_ref, cbuf_ref, h0_ref, cw_ref, cb_ref, wa_ref, ba_ref, wx_ref, bx_ref,
                  lam_ref, r_ref, hlast_ref, tail_ref, ubuf, tail_s, hcar):
    @pl.when(pl.program_id(1) == 0)
    def _():
        tail_s[...] = cbuf_ref[0]
        hcar[...] = h0_ref[0]

    r, new_tail, h = _rglru_tile(u_ref[0], g_ref[0], tail_s[...], hcar[...], ubuf, cw_ref, cb_ref,
                                 wa_ref, ba_ref, wx_ref, bx_ref, lam_ref)
    tail_s[...] = new_tail
    hcar[...] = h
    r_ref[0] = r
    hlast_ref[0] = h
    tail_ref[0] = new_tail


def _rglru(u, g, conv_buf, h0, conv_w, conv_b, wa_bd, b_a, wx_bd, b_x, lam, *, tc):
    b, t, r = u.shape
    tc = min(tc, t)
    pad = SUBLANES
    cbuf = jnp.pad(conv_buf, ((0, 0), (pad - (CONV_WIDTH - 1), 0), (0, 0)))
    chunk = pl.BlockSpec((1, tc, r), lambda bi, ti: (bi, ti, 0))
    per_seq = lambda rows: pl.BlockSpec((1, rows, r), lambda bi, ti: (bi, 0, 0))
    vec = _resident((1, r))
    out, h_last, tail = pl.pallas_call(
        _rglru_kernel,
        grid=(b, t // tc),
        in_specs=[chunk, chunk, per_seq(pad), per_seq(1), _resident((CONV_WIDTH, r)), vec,
                  _resident((r, r)), vec, _resident((r, r)), vec, vec],
        out_specs=[chunk, per_seq(1), per_seq(pad)],
        out_shape=[jax.ShapeDtypeStruct((b, t, r), BF16),
                   jax.ShapeDtypeStruct((b, 1, r), F32),
                   jax.ShapeDtypeStruct((b, pad, r), F32)],
        scratch_shapes=[pltpu.VMEM((pad + tc, r), F32), pltpu.VMEM((pad, r), F32), pltpu.VMEM((1, r), F32)],
        compiler_params=pltpu.CompilerParams(dimension_semantics=("arbitrary", "arbitrary"),
                                             vmem_limit_bytes=VMEM_LIMIT),
        name="rglru",
    )(u, g, cbuf, h0.reshape(b, 1, r), conv_w, conv_b.reshape(1, r), wa_bd, b_a.reshape(1, r),
      wx_bd, b_x.reshape(1, r), lam.reshape(1, r))
    return out, h_last.reshape(b, r), tail[:, pad - (CONV_WIDTH - 1):]


def _out_ffn_kernel(x1_ref, attn_ref, r_ref, wo_ref, ln2_ref, wg_ref, wu_ref, wd_ref, lnf_ref, y_ref,
                    *, final_norm):
    mix = jnp.concatenate([attn_ref[...].astype(BF16), r_ref[...]], axis=-1)
    x2 = x1_ref[...] + jnp.dot(mix, wo_ref[...], preferred_element_type=F32)
    xn = _rms(x2, ln2_ref[...]).astype(BF16)
    x3 = x2 + 0.5 * _swiglu(xn, wg_ref, wu_ref, wd_ref)
    y_ref[...] = _rms(x3, lnf_ref[...]) if final_norm else x3


def _out_ffn(x1, attn, r, wo, ln2, wg, wu, wd, lnf, *, final_norm, tm):
    n, dm = x1.shape
    dff = wg.shape[1]
    row = lambda w: pl.BlockSpec((tm, w), lambda i: (i, 0))
    return pl.pallas_call(
        functools.partial(_out_ffn_kernel, final_norm=final_norm),
        grid=(n // tm,),
        in_specs=[row(dm), row(D_ATTN), row(r.shape[1]), _resident(wo.shape), _resident((1, dm)),
                  _resident((dm, dff)), _resident((dm, dff)), _resident((dff, dm)), _resident((1, dm))],
        out_specs=row(dm),
        out_shape=jax.ShapeDtypeStruct((n, dm), F32),
        compiler_params=pltpu.CompilerParams(dimension_semantics=("arbitrary",),
                                             vmem_limit_bytes=VMEM_LIMIT),
        name="out_ffn",
    )(x1, attn, r, wo, ln2, wg, wu, wd, lnf)


def _block_diag(w):
    nb, c, d = w.shape
    return jnp.einsum("ncd,nm->ncmd", w, jnp.eye(nb, dtype=w.dtype)).reshape(nb * c, nb * d)


def _layer(x, conv_buf, h0, k_past, v_past, lw, lnf, *, final_norm, tm, tc):
    (ln1, w1g, w1u, w1d, ln_m, w_in, conv_w, conv_b, wa_bd, b_a, wx_bd, b_x, lam,
     w_out, ln2, w2g, w2u, w2d) = lw
    b, t, dm = x.shape
    r = conv_w.shape[-1]
    prompt = k_past is None
    keep = min(MAX_WINDOW, t) if prompt else t
    shp = lambda a: a.reshape(b, t, a.shape[-1])
    rnn_w = (conv_w, conv_b, wa_bd, b_a, wx_bd, b_x, lam)
    if prompt:
        x1, q, k, v, kt, vt, rnn, h_last, new_buf = _ffn_in_rglru(
            x.reshape(b * t, dm), ln1, w1g, w1u, w1d, ln_m, w_in, conv_buf, h0, *rnn_w, seq=t, keep=keep, tm=tm)
        attn = _attn_prompt(shp(q), shp(k), shp(v))
        k_state = jnp.transpose(kt.reshape(b, N_HEADS, HEAD_DIM, keep), (0, 3, 1, 2))
        v_state = jnp.transpose(vt.reshape(b, N_HEADS, HEAD_DIM, keep), (0, 3, 1, 2))
    else:
        x1, q, k, v, u, g = _ffn_in(x.reshape(b * t, dm), ln1, w1g, w1u, w1d, ln_m, w_in, tm=tm)
        attn = _attn_sample(shp(q), shp(k), shp(v),
                            jnp.transpose(k_past, (0, 2, 3, 1)), jnp.transpose(v_past, (0, 2, 3, 1)))
        rnn, h_last, new_buf = _rglru(shp(u), shp(g), conv_buf, h0, *rnn_w, tc=tc)
        k_state = shp(k).reshape(b, t, N_HEADS, HEAD_DIM)
        v_state = shp(v).reshape(b, t, N_HEADS, HEAD_DIM)
    y = _out_ffn(x1, attn.reshape(b * t, D_ATTN), rnn.reshape(b * t, r), w_out, ln2, w2g, w2u, w2d, lnf,
                 final_norm=final_norm, tm=tm)
    return y.reshape(b, t, dm), k_state, v_state, h_last, new_buf


def kernel(x_prompt, x_sample, cache_k_win, cache_v_win, state_lru_h, state_lru_conv, ln_ffn1, w_ffn1_gate, w_ffn1_up, w_ffn1_down, ln_mix, w_in, conv_w, conv_b, w_gate_a, b_gate_a, w_gate_x, b_gate_x, lru_lambda, w_out, ln_ffn2, w_ffn2_gate, w_ffn2_up, w_ffn2_down, ln_final):
    depth = ln_ffn1.shape[0]
    dm = x_prompt.shape[-1]
    d_rnn = conv_w.shape[-1]
    bp = x_prompt.shape[0]
    xp, xs = x_prompt, x_sample
    outs = [[] for _ in range(8)]
    lnf = ln_final.reshape(1, dm)
    cfg = dict(tm=256, tc=512)
    for l in range(depth):
        lw = (ln_ffn1[l].reshape(1, dm), w_ffn1_gate[l].astype(BF16), w_ffn1_up[l].astype(BF16),
              w_ffn1_down[l].astype(BF16), ln_mix[l].reshape(1, dm), w_in[l].astype(BF16),
              conv_w[l], conv_b[l], _block_diag(w_gate_a[l]).astype(BF16), b_gate_a[l].reshape(-1),
              _block_diag(w_gate_x[l]).astype(BF16), b_gate_x[l].reshape(-1), lru_lambda[l],
              w_out[l].astype(BF16), ln_ffn2[l].reshape(1, dm), w_ffn2_gate[l].astype(BF16),
              w_ffn2_up[l].astype(BF16), w_ffn2_down[l].astype(BF16))
        last = l == depth - 1
        zero_buf = jnp.zeros((bp, CONV_WIDTH - 1, d_rnn), xp.dtype)
        zero_h = jnp.zeros((bp, d_rnn), xp.dtype)
        xp, kp, vp, hp, cp = _layer(xp, zero_buf, zero_h, None, None, lw, lnf, final_norm=last, **cfg)
        xs, kn, vn, hn, cn = _layer(xs, state_lru_conv[l], state_lru_h[l], cache_k_win[l],
                                    cache_v_win[l], lw, lnf, final_norm=last, **cfg)
        for lst, val in zip(outs, (kp, vp, hp, cp, kn, vn, hn, cn)):
            lst.append(val)
    return (xp, xs) + tuple(jnp.stack(lst) for lst in outs)
```

```python
import functools

import jax
import jax.numpy as jnp
import numpy as np
from jax.experimental import pallas as pl
from jax.experimental.pallas import tpu as pltpu

N_HEADS = 8
HEAD_DIM = 64
D_ATTN = N_HEADS * HEAD_DIM
CONV_WIDTH = 4
LRU_C = 8.0
DILATED = ((128, 1), (512, 4), (2048, 16))
MAX_WINDOW = 2048
BLK = 128
EPS = 1e-6

LANES = 128
SUBLANES = 8
VMEM_LIMIT = 56 * 1024 * 1024
VMEM_LIMIT_ATTN = 60 * 1024 * 1024

F32 = jnp.float32
BF16 = jnp.bfloat16
NEG_INF = float("-inf")


def _rms(x, g):
    return x * jax.lax.rsqrt(jnp.mean(x * x, axis=-1, keepdims=True) + EPS) * g


def _swiglu(xn, wg_ref, wu_ref, wd_ref):
    gate = jnp.dot(xn, wg_ref[...], preferred_element_type=F32)
    up = jnp.dot(xn, wu_ref[...], preferred_element_type=F32)
    h = (gate * jax.nn.sigmoid(gate) * up).astype(BF16)
    return jnp.dot(h, wd_ref[...], preferred_element_type=F32)


def _resident(shape):
    return pl.BlockSpec(shape, lambda *_: (0,) * len(shape), pipeline_mode=pl.Buffered(1))


def _ffn_in_body(x_ref, ln1_ref, wg_ref, wu_ref, wd_ref, lnm_ref, win_ref, x1_ref, q_ref, k_ref, v_ref):
    x = x_ref[...]
    xn = _rms(x, ln1_ref[...]).astype(BF16)
    x1 = x + 0.5 * _swiglu(xn, wg_ref, wu_ref, wd_ref)
    x1_ref[...] = x1
    zn = _rms(x1, lnm_ref[...]).astype(BF16)
    z = jnp.dot(zn, win_ref[...], preferred_element_type=F32)
    d = D_ATTN
    q_ref[...] = z[:, :d] * (HEAD_DIM ** -0.5)
    k_ref[...] = z[:, d:2 * d]
    v_ref[...] = z[:, 2 * d:3 * d]
    r = (z.shape[1] - 3 * d) // 2
    return z[:, 3 * d:3 * d + r], z[:, 3 * d + r:]


def _ffn_in_kernel(x_ref, ln1_ref, wg_ref, wu_ref, wd_ref, lnm_ref, win_ref,
                   x1_ref, q_ref, k_ref, v_ref, u_ref, g_ref):
    u, g = _ffn_in_body(x_ref, ln1_ref, wg_ref, wu_ref, wd_ref, lnm_ref, win_ref, x1_ref, q_ref, k_ref, v_ref)
    u_ref[...] = u
    g_ref[...] = g


def _ffn_in_rglru_kernel(x_ref, ln1_ref, wg_ref, wu_ref, wd_ref, lnm_ref, win_ref,
                         cbuf_ref, h0_ref, cw_ref, cb_ref, wa_ref, ba_ref, wx_ref, bx_ref, lam_ref,
                         x1_ref, q_ref, k_ref, v_ref, kt_ref, vt_ref, r_ref, hlast_ref, tail_ref,
                         u_s, g_s, tail_s, hcar, *, tiles_per_seq):
    i = pl.program_id(0)

    @pl.when(i == 0)
    def _():
        for ref in (u_s, g_s, tail_s, hcar):
            ref[...] = jnp.zeros_like(ref)

    starts = (i - 1) % tiles_per_seq == 0
    tail = jnp.where(starts, cbuf_ref[0], tail_s[...])
    h_prev = jnp.where(starts, h0_ref[0], hcar[...])
    r, new_tail, h = _rglru_tile(u_s[...], g_s[...], tail, h_prev, cw_ref, cb_ref, wa_ref, ba_ref,
                                 wx_ref, bx_ref, lam_ref)
    u, g = _ffn_in_body(x_ref, ln1_ref, wg_ref, wu_ref, wd_ref, lnm_ref, win_ref, x1_ref, q_ref, k_ref, v_ref)
    kt_ref[0] = k_ref[...].T
    vt_ref[0] = v_ref[...].T
    tail_s[...] = new_tail
    hcar[...] = h
    r_ref[...] = r
    hlast_ref[0] = h
    tail_ref[0] = new_tail
    u_s[...] = u
    g_s[...] = g


def _ffn_in(x, ln1, wg, wu, wd, lnm, win, *, tm):
    n, dm = x.shape
    dff = wg.shape[1]
    d_rnn = (win.shape[1] - 3 * D_ATTN) // 2
    row = lambda w: pl.BlockSpec((tm, w), lambda i: (i, 0))
    widths = (dm, D_ATTN, D_ATTN, D_ATTN, d_rnn, d_rnn)
    return pl.pallas_call(
        _ffn_in_kernel,
        grid=(n // tm,),
        in_specs=[row(dm), _resident((1, dm)), _resident((dm, dff)), _resident((dm, dff)),
                  _resident((dff, dm)), _resident((1, dm)), _resident(win.shape)],
        out_specs=[row(w) for w in widths],
        out_shape=[jax.ShapeDtypeStruct((n, w), F32) for w in widths],
        compiler_params=pltpu.CompilerParams(dimension_semantics=("arbitrary",),
                                             vmem_limit_bytes=VMEM_LIMIT),
        name="ffn_in",
    )(x, ln1, wg, wu, wd, lnm, win)


def _ffn_in_rglru(x, ln1, wg, wu, wd, lnm, win, conv_buf, h0, conv_w, conv_b, wa_bd, b_a, wx_bd, b_x, lam,
                  *, seq, keep, tm):
    n, dm = x.shape
    dff = wg.shape[1]
    r = conv_w.shape[-1]
    nb = n // seq
    tiles = n // tm
    tps = seq // tm
    first_kept = tps - keep // tm
    pad = SUBLANES
    cbuf = jnp.pad(conv_buf, ((0, 0), (pad - (CONV_WIDTH - 1), 0), (0, 0)))
    cur_tile = lambda i: jnp.minimum(i, tiles - 1)
    cur = lambda w: pl.BlockSpec((tm, w), lambda i: (cur_tile(i), 0))
    kept = pl.BlockSpec((1, D_ATTN, tm), lambda i: (cur_tile(i) // tps, 0,
                                                    jnp.maximum(cur_tile(i) % tps - first_kept, 0)))
    prev = lambda w: pl.BlockSpec((tm, w), lambda i: (jnp.maximum(i - 1, 0), 0))
    prev_seq = lambda rows: pl.BlockSpec((1, rows, r), lambda i: (jnp.maximum(i - 1, 0) // tps, 0, 0))
    vec = _resident((1, r))
    x1, q, k, v, kt, vt, rnn, h_last, tail = pl.pallas_call(
        functools.partial(_ffn_in_rglru_kernel, tiles_per_seq=tps),
        grid=(tiles + 1,),
        in_specs=[cur(dm), _resident((1, dm)), _resident((dm, dff)), _resident((dm, dff)),
                  _resident((dff, dm)), _resident((1, dm)), _resident(win.shape),
                  prev_seq(pad), prev_seq(1), _resident((CONV_WIDTH, r)), vec,
                  _resident((r, r)), vec, _resident((r, r)), vec, vec],
        out_specs=[cur(dm), cur(D_ATTN), cur(D_ATTN), cur(D_ATTN), kept, kept,
                   prev(r), prev_seq(1), prev_seq(pad)],
        out_shape=[jax.ShapeDtypeStruct((n, dm), F32)] + [jax.ShapeDtypeStruct((n, D_ATTN), F32)] * 3
                  + [jax.ShapeDtypeStruct((nb, D_ATTN, keep), F32)] * 2
                  + [jax.ShapeDtypeStruct((n, r), BF16), jax.ShapeDtypeStruct((nb, 1, r), F32),
                     jax.ShapeDtypeStruct((nb, pad, r), F32)],
        scratch_shapes=[pltpu.VMEM((tm, r), F32), pltpu.VMEM((tm, r), F32),
                        pltpu.VMEM((pad, r), F32), pltpu.VMEM((1, r), F32)],
        compiler_params=pltpu.CompilerParams(dimension_semantics=("arbitrary",),
                                             vmem_limit_bytes=VMEM_LIMIT),
        name="ffn_in_rglru",
    )(x, ln1, wg, wu, wd, lnm, win, cbuf, h0.reshape(nb, 1, r), conv_w, conv_b.reshape(1, r),
      wa_bd, b_a.reshape(1, r), wx_bd, b_x.reshape(1, r), lam.reshape(1, r))
    return x1, q, k, v, kt, vt, rnn, h_last.reshape(nb, r), tail[:, pad - (CONV_WIDTH - 1):]


FOLD = 16
BLOCK_UNROLL = 16


def _fold_rows(s):
    return s // FOLD + SUBLANES


def _branch_bias(dil):
    nch = FOLD // dil
    qs = BLK // nch
    ip = np.arange(BLK)
    kp = np.arange(2 * BLK)
    i = nch * (ip % qs) + ip // qs
    kk = nch * (kp % (2 * qs)) + kp // (2 * qs)
    delta = kk[None, :] - i[:, None]
    first = delta <= 0
    other = (delta >= 0) & (delta <= BLK)
    return np.where(np.concatenate([first, other], axis=0), 0.0, NEG_INF).astype(np.float32)


def _attn_prompt_kernel(q_ref, k_ref, v_ref, bias_ref, o_ref, qf, kf, vf, o_acc, m_acc, l_acc):
    s = q_ref.shape[1]
    pr = o_acc.shape[0] // FOLD
    prb = qf.shape[0] // FOLD
    tile = 2 * SUBLANES
    nt = (((1,), (1,)), ((), ()))
    low = jax.lax.broadcasted_iota(jnp.int32, (BLK, LANES), 1) < HEAD_DIM
    log2e = float(np.log2(np.e))

    def fold(mi, c):
        for grp in range(FOLD // SUBLANES):
            src = pl.ds(pl.multiple_of(mi * FOLD + grp * SUBLANES, SUBLANES), SUBLANES)
            dst = pl.ds(mi + grp * SUBLANES * pr, SUBLANES, stride=pr)
            o_acc[dst, :] = q_ref[0, src, :]
            m_acc[dst, :] = k_ref[0, src, :]
            l_acc[dst, :] = v_ref[0, src, :]
        return c

    jax.lax.fori_loop(0, s // FOLD, fold, 0, unroll=8)

    def pack(r, c):
        for ch in range(s // FOLD // BLK):
            src = pl.ds(pl.multiple_of(r * pr + ch * BLK, SUBLANES), BLK)
            dst = pl.ds(pl.multiple_of(r * prb + ch * BLK, tile), BLK)
            qf[dst, :] = (o_acc[src, :] * log2e).astype(BF16)
            kf[dst, :] = m_acc[src, :].astype(BF16)
            vf[dst, :] = l_acc[src, :].astype(BF16)
        return c

    jax.lax.fori_loop(0, FOLD, pack, 0)

    n_branches = len(DILATED)
    for bi, (_, dil) in enumerate(DILATED):
        natural = dil == 1
        nch = 1 if natural else FOLD // dil
        qs = BLK // nch
        nblk = s // (dil * BLK)
        first, last = bi == 0, bi == n_branches - 1
        assert first or not natural

        def block(idx, c, bi=bi, dil=dil, natural=natural, nch=nch, qs=qs, nblk=nblk, first=first, last=last):
            rd = idx // nblk
            j = idx % nblk
            prev = jnp.maximum(j - 1, 0)

            def gather(ref, base, n):
                if natural:
                    return ref[0, pl.ds(pl.multiple_of(base, BLK), n), :]
                return jnp.concatenate(
                    [ref[pl.ds(pl.multiple_of(rd * prb + base + a * dil * prb, tile), n), :] for a in range(nch)],
                    axis=0)

            if natural:
                q2 = (gather(q_ref, qs * j, qs) * log2e).astype(BF16)
                k2 = gather(k_ref, qs * prev, 2 * qs).astype(BF16)
                v2 = gather(v_ref, qs * prev, 2 * qs).astype(BF16)
            else:
                q2 = gather(qf, qs * j, qs)
                k2 = gather(kf, qs * prev, 2 * qs)
                v2 = gather(vf, qs * prev, 2 * qs)
            brow = (2 * bi + jnp.minimum(j, 1)) * BLK
            bias = bias_ref[pl.ds(pl.multiple_of(brow, BLK), BLK), :]
            res = []
            for sel in (low, ~low):
                qm = jnp.where(sel, q2, jnp.zeros_like(q2))
                sc = jax.lax.dot_general(qm, k2, nt, preferred_element_type=F32) + bias
                m = jnp.max(sc, axis=-1, keepdims=True)
                p = jnp.exp2(sc - m)
                l = jnp.sum(p, axis=-1, keepdims=True)
                o = jnp.dot(p.astype(BF16), v2, preferred_element_type=F32)
                res.append((o, m, l))
            o_b, m_b, l_b = (jnp.where(low, x0, x1) for x0, x1 in zip(*res))
            if natural:
                for g in range(BLK // SUBLANES):
                    rows = pl.ds((g % 2) * SUBLANES * pr + (BLK // FOLD) * j + g // 2, SUBLANES, stride=pr)
                    part = slice(g * SUBLANES, (g + 1) * SUBLANES)
                    o_acc[rows, :] = o_b[part]
                    m_acc[rows, :] = m_b[part]
                    l_acc[rows, :] = l_b[part]
                return c
            for a in range(nch):
                rows = pl.ds(pl.multiple_of(rd * pr + qs * j + a * dil * pr, SUBLANES), qs)
                part = slice(a * qs, (a + 1) * qs)
                m_old = m_acc[rows, :]
                m_new = jnp.maximum(m_old, m_b[part])
                keep = m_old >= m_b[part]
                e_min = jnp.exp2(jnp.minimum(m_old, m_b[part]) - m_new)
                e_old = jnp.where(keep, 1.0, e_min)
                e_b = jnp.where(keep, e_min, 1.0)
                o_new = o_acc[rows, :] * e_old + o_b[part] * e_b
                l_new = l_acc[rows, :] * e_old + l_b[part] * e_b
                if last:
                    o_acc[rows, :] = o_new * (1.0 / l_new)
                else:
                    o_acc[rows, :] = o_new
                    m_acc[rows, :] = m_new
                    l_acc[rows, :] = l_new
            return c

        jax.lax.fori_loop(0, dil * nblk, block, 0, unroll=BLOCK_UNROLL)

    def unfold(mi, c):
        rows = o_acc[pl.ds(mi, FOLD, stride=pr), :]
        o_ref[0, pl.ds(pl.multiple_of(mi * FOLD, FOLD), FOLD), :] = rows.astype(BF16)
        return c

    jax.lax.fori_loop(0, s // FOLD, unfold, 0, unroll=4)


def _attn_prompt(q, k, v):
    b, s, d = q.shape
    assert s % (FOLD * BLK) == 0 and s // FOLD >= 2 * BLK and all(FOLD % dil == 0 for _, dil in DILATED)
    assert DILATED[0][1] == 1 and FOLD == 2 * SUBLANES
    bias = np.concatenate([_branch_bias(FOLD if dil == 1 else dil) for _, dil in DILATED], axis=0)
    seq = pl.BlockSpec((1, s, LANES), lambda bi, hp: (bi, 0, hp))
    folded = pltpu.VMEM((FOLD * (s // FOLD + 2 * SUBLANES), LANES), BF16)
    acc = pltpu.VMEM((FOLD * _fold_rows(s), LANES), F32)
    return pl.pallas_call(
        _attn_prompt_kernel,
        grid=(b, d // LANES),
        in_specs=[seq, seq, seq, _resident(bias.shape)],
        out_specs=pl.BlockSpec((1, s, LANES), lambda bi, hp: (bi, 0, hp)),
        out_shape=jax.ShapeDtypeStruct((b, s, d), BF16),
        scratch_shapes=[folded] * 3 + [acc] * 3,
        compiler_params=pltpu.CompilerParams(dimension_semantics=("arbitrary", "arbitrary"),
                                             vmem_limit_bytes=VMEM_LIMIT_ATTN),
        name="attn_prompt",
    )(q, k, v, jnp.asarray(bias))


def _branch_counts(t, w_buf):
    i = np.arange(t)[:, None]
    c = np.arange(w_buf)[None, :]
    n = np.arange(LANES)[None, :]
    cnt_c = np.zeros((t, w_buf), np.float32)
    cnt_n = np.zeros((t, LANES), np.float32)
    for win, dil in DILATED:
        dist = w_buf + i - c
        cnt_c += ((dist % dil == 0) & (dist >= dil) & (dist <= win)).astype(np.float32)
        dist = i - n
        cnt_n += ((n < t) & (dist >= 0) & (dist % dil == 0) & (dist <= win)).astype(np.float32)
    return np.concatenate([cnt_c, cnt_n], axis=1)


def _attn_sample_kernel(q_ref, knt_ref, vnt_ref, kc_ref, vc_ref, cnt_ref, ot_ref, knp, vnp, *, t):
    rows = 2 * SUBLANES

    @pl.when(pl.program_id(0) == 0)
    def _():
        knp[...] = jnp.zeros_like(knp)
        vnp[...] = jnp.zeros_like(vnp)

    knp[:, 0:t] = knt_ref[0]
    vnp[:, 0:t] = vnt_ref[0]
    cnt = cnt_ref[...]
    valid = cnt > 0.0
    q = jnp.concatenate([q_ref[0], jnp.zeros((rows - t, D_ATTN), F32)], axis=0).astype(BF16)
    ones = jnp.ones((SUBLANES, cnt.shape[1]), F32)
    nt = (((1,), (1,)), ((), ()))
    for h in range(N_HEADS):
        hrows = slice(h * HEAD_DIM, (h + 1) * HEAD_DIM)
        kt = jnp.concatenate([kc_ref[0, h], knp[hrows, :]], axis=1).astype(BF16)
        sc = jnp.dot(q[:, hrows], kt, preferred_element_type=F32)
        sc = jnp.where(valid, sc, NEG_INF)
        m = jnp.maximum(jnp.max(sc, axis=-1, keepdims=True), -1e30)
        p = (cnt * jnp.exp(sc - m)).astype(BF16)
        vt = jnp.concatenate([vc_ref[0, h], vnp[hrows, :]], axis=1)
        vt = jnp.concatenate([vt, ones], axis=0).astype(BF16)
        o = jax.lax.dot_general(vt, p, nt, preferred_element_type=F32)
        ot_ref[0, hrows, :] = (o[0:HEAD_DIM] * (1.0 / o[HEAD_DIM:HEAD_DIM + 1]))[:, 0:t]


def _attn_sample(q, k_new, v_new, k_cache_t, v_cache_t):
    b, t, d = q.shape
    w = k_cache_t.shape[3]
    assert t <= SUBLANES and w % LANES == 0
    cnt = np.zeros((2 * SUBLANES, w + LANES), np.float32)
    cnt[:t] = _branch_counts(t, w)
    tr = lambda a: jnp.swapaxes(a, 1, 2)
    tok = pl.BlockSpec((1, t, d), lambda bi: (bi, 0, 0))
    tok_t = pl.BlockSpec((1, d, t), lambda bi: (bi, 0, 0))
    cache = pl.BlockSpec((1,) + k_cache_t.shape[1:], lambda bi: (bi, 0, 0, 0))
    out = pl.pallas_call(
        functools.partial(_attn_sample_kernel, t=t),
        grid=(b,),
        in_specs=[tok, tok_t, tok_t, cache, cache, _resident(cnt.shape)],
        out_specs=tok_t,
        out_shape=jax.ShapeDtypeStruct((b, d, t), F32),
        scratch_shapes=[pltpu.VMEM((d, LANES), F32), pltpu.VMEM((d, LANES), F32)],
        compiler_params=pltpu.CompilerParams(dimension_semantics=("arbitrary",),
                                             vmem_limit_bytes=VMEM_LIMIT),
        name="attn_sample",
    )(q, tr(k_new), tr(v_new), k_cache_t, v_cache_t, jnp.asarray(cnt))
    return tr(out)


def _rglru_coeffs(xc, wa_ref, ba_ref, wx_ref, bx_ref, lam_ref):
    xb = xc.astype(BF16)
    rg = jax.nn.sigmoid(jnp.dot(xb, wa_ref[...], preferred_element_type=F32) + ba_ref[...])
    ig = jax.nn.sigmoid(jnp.dot(xb, wx_ref[...], preferred_element_type=F32) + bx_ref[...])
    log_a = (-LRU_C * jax.nn.softplus(-lam_ref[...])) * rg
    a = jnp.exp(log_a)
    return a, jnp.sqrt(-jnp.tanh(log_a) * (a * a + 1.0)) * ig * xc


def _rglru_tile(u, g, tail, h_prev, cw_ref, cb_ref, wa_ref, ba_ref, wx_ref, bx_ref, lam_ref):
    tc = u.shape[0]
    pad = SUBLANES
    full = jnp.concatenate([tail, u], axis=0)
    first = pad - (CONV_WIDTH - 1)
    xc = cb_ref[...] + cw_ref[0:1, :] * full[first:first + tc]
    for j in range(1, CONV_WIDTH):
        xc = xc + cw_ref[j:j + 1, :] * full[first + j:first + j + tc]
    new_tail = full[tc:tc + pad]
    a, b = _rglru_coeffs(xc, wa_ref, ba_ref, wx_ref, bx_ref, lam_ref)

    row = jax.lax.broadcasted_iota(jnp.int32, (SUBLANES, u.shape[1]), 0)
    h = h_prev
    hs = []
    for gi in range(tc // SUBLANES):
        ag = a[gi * SUBLANES:(gi + 1) * SUBLANES]
        bg = b[gi * SUBLANES:(gi + 1) * SUBLANES]
        for sh in (1, 2, 4):
            a_prev = pltpu.roll(ag, sh, axis=0)
            b_prev = pltpu.roll(bg, sh, axis=0)
            take = row >= sh
            bg = jnp.where(take, ag * b_prev + bg, bg)
            ag = jnp.where(take, ag * a_prev, ag)
        hg = ag * h + bg
        hs.append(hg)
        h = hg[SUBLANES - 1:SUBLANES, :]
    r = (jnp.concatenate(hs, axis=0) * jax.nn.gelu(g)).astype(BF16)
    return r, new_tail, h


def _rglru_steps_kernel(u_ref, g_ref, cbuf_ref, h0_ref, cw_ref, cb_ref, wa_ref, ba_ref, wx_ref, bx_ref,
                        lam_ref, r_ref, hlast_ref, tail_ref, *, t):
    nb = h0_ref.shape[0]
    taps = CONV_WIDTH - 1
    step_rows = lambda ti: pl.ds(ti, nb, stride=t)
    full = [cbuf_ref[j] for j in range(taps)] + [u_ref[step_rows(ti), :] for ti in range(t)]
    xcs = []
    for ti in range(t):
        xc = cb_ref[...] + cw_ref[0:1, :] * full[ti]
        for j in range(1, CONV_WIDTH):
            xc = xc + cw_ref[j:j + 1, :] * full[ti + j]
        xcs.append(xc)
    a, b = _rglru_coeffs(jnp.concatenate(xcs, axis=0), wa_ref, ba_ref, wx_ref, bx_ref, lam_ref)
    h = h0_ref[...]
    for ti in range(t):
        rows = slice(ti * nb, (ti + 1) * nb)
        h = a[rows] * h + b[rows]
        r_ref[step_rows(ti), :] = h * jax.nn.gelu(g_ref[step_rows(ti), :])
    hlast_ref[...] = h
    for j in range(taps):
        tail_ref[j] = full[t + j]


def _rglru_steps(u, g, conv_buf, h0, conv_w, conv_b, wa_bd, b_a, wx_bd, b_x, lam, *, t):
    n, r = u.shape
    nb = n // t
    taps = CONV_WIDTH - 1
    assert t >= taps
    col = lambda rows: pl.BlockSpec((rows, LANES), lambda c: (0, c))
    state = pl.BlockSpec((taps, nb, LANES), lambda c: (0, 0, c))
    diag = pl.BlockSpec((LANES, LANES), lambda c: (c, c))
    out, h_last, tail = pl.pallas_call(
        functools.partial(_rglru_steps_kernel, t=t),
        grid=(r // LANES,),
        in_specs=[col(n), col(n), state, col(nb), col(CONV_WIDTH), col(1), diag, col(1), diag, col(1), col(1)],
        out_specs=[col(n), col(nb), state],
        out_shape=[jax.ShapeDtypeStruct((n, r), F32), jax.ShapeDtypeStruct((nb, r), F32),
                   jax.ShapeDtypeStruct((taps, nb, r), F32)],
        compiler_params=pltpu.CompilerParams(dimension_semantics=("arbitrary",),
                                             vmem_limit_bytes=VMEM_LIMIT),
        name="rglru_steps",
    )(u, g, jnp.swapaxes(conv_buf, 0, 1), h0, conv_w, conv_b.reshape(1, r), wa_bd, b_a.reshape(1, r),
      wx_bd, b_x.reshape(1, r), lam.reshape(1, r))
    return out, h_last, jnp.swapaxes(tail, 0, 1)


def _out_ffn_kernel(x1_ref, attn_ref, r_ref, wo_ref, ln2_ref, wg_ref, wu_ref, wd_ref, lnf_ref, y_ref,
                    *, final_norm):
    mix = jnp.concatenate([attn_ref[...].astype(BF16), r_ref[...].astype(BF16)], axis=-1)
    x2 = x1_ref[...] + jnp.dot(mix, wo_ref[...], preferred_element_type=F32)
    xn = _rms(x2, ln2_ref[...]).astype(BF16)
    x3 = x2 + 0.5 * _swiglu(xn, wg_ref, wu_ref, wd_ref)
    y_ref[...] = _rms(x3, lnf_ref[...]) if final_norm else x3


def _out_ffn(x1, attn, r, wo, ln2, wg, wu, wd, lnf, *, final_norm, tm):
    n, dm = x1.shape
    dff = wg.shape[1]
    row = lambda w: pl.BlockSpec((tm, w), lambda i: (i, 0))
    return pl.pallas_call(
        functools.partial(_out_ffn_kernel, final_norm=final_norm),
        grid=(n // tm,),
        in_specs=[row(dm), row(D_ATTN), row(r.shape[1]), _resident(wo.shape), _resident((1, dm)),
                  _resident((dm, dff)), _resident((dm, dff)), _resident((dff, dm)), _resident((1, dm))],
        out_specs=row(dm),
        out_shape=jax.ShapeDtypeStruct((n, dm), F32),
        compiler_params=pltpu.CompilerParams(dimension_semantics=("arbitrary",),
                                             vmem_limit_bytes=VMEM_LIMIT),
        name="out_ffn",
    )(x1, attn, r, wo, ln2, wg, wu, wd, lnf)


def _block_diag(w):
    nb, c, d = w.shape
    return jnp.einsum("ncd,nm->ncmd", w, jnp.eye(nb, dtype=w.dtype)).reshape(nb * c, nb * d)


def _layer(x, conv_buf, h0, k_past, v_past, lw, lnf, *, final_norm, tm):
    (ln1, w1g, w1u, w1d, ln_m, w_in, conv_w, conv_b, wa_bd, b_a, wx_bd, b_x, lam,
     w_out, ln2, w2g, w2u, w2d) = lw
    b, t, dm = x.shape
    r = conv_w.shape[-1]
    prompt = k_past is None
    keep = min(MAX_WINDOW, t) if prompt else t
    shp = lambda a: a.reshape(b, t, a.shape[-1])
    rnn_w = (conv_w, conv_b, wa_bd, b_a, wx_bd, b_x, lam)
    if prompt:
        x1, q, k, v, kt, vt, rnn, h_last, new_buf = _ffn_in_rglru(
            x.reshape(b * t, dm), ln1, w1g, w1u, w1d, ln_m, w_in, conv_buf, h0, *rnn_w, seq=t, keep=keep, tm=tm)
        attn = _attn_prompt(shp(q), shp(k), shp(v))
        k_state = jnp.transpose(kt.reshape(b, N_HEADS, HEAD_DIM, keep), (0, 3, 1, 2))
        v_state = jnp.transpose(vt.reshape(b, N_HEADS, HEAD_DIM, keep), (0, 3, 1, 2))
    else:
        x1, q, k, v, u, g = _ffn_in(x.reshape(b * t, dm), ln1, w1g, w1u, w1d, ln_m, w_in, tm=tm)
        attn = _attn_sample(shp(q), shp(k), shp(v),
                            jnp.transpose(k_past, (0, 2, 3, 1)), jnp.transpose(v_past, (0, 2, 3, 1)))
        rnn, h_last, new_buf = _rglru_steps(u, g, conv_buf, h0, *rnn_w, t=t)
        k_state = shp(k).reshape(b, t, N_HEADS, HEAD_DIM)
        v_state = shp(v).reshape(b, t, N_HEADS, HEAD_DIM)
    y = _out_ffn(x1, attn.reshape(b * t, D_ATTN), rnn.reshape(b * t, r), w_out, ln2, w2g, w2u, w2d, lnf,
                 final_norm=final_norm, tm=tm)
    return y.reshape(b, t, dm), k_state, v_state, h_last, new_buf


def kernel(x_prompt, x_sample, cache_k_win, cache_v_win, state_lru_h, state_lru_conv, ln_ffn1, w_ffn1_gate, w_ffn1_up, w_ffn1_down, ln_mix, w_in, conv_w, conv_b, w_gate_a, b_gate_a, w_gate_x, b_gate_x, lru_lambda, w_out, ln_ffn2, w_ffn2_gate, w_ffn2_up, w_ffn2_down, ln_final):
    depth = ln_ffn1.shape[0]
    dm = x_prompt.shape[-1]
    d_rnn = conv_w.shape[-1]
    bp = x_prompt.shape[0]
    xp, xs = x_prompt, x_sample
    outs = [[] for _ in range(8)]
    lnf = ln_final.reshape(1, dm)
    cfg = dict(tm=256)
    for l in range(depth):
        lw = (ln_ffn1[l].reshape(1, dm), w_ffn1_gate[l].astype(BF16), w_ffn1_up[l].astype(BF16),
              w_ffn1_down[l].astype(BF16), ln_mix[l].reshape(1, dm), w_in[l].astype(BF16),
              conv_w[l], conv_b[l], _block_diag(w_gate_a[l]).astype(BF16), b_gate_a[l].reshape(-1),
              _block_diag(w_gate_x[l]).astype(BF16), b_gate_x[l].reshape(-1), lru_lambda[l],
              w_out[l].astype(BF16), ln_ffn2[l].reshape(1, dm), w_ffn2_gate[l].astype(BF16),
              w_ffn2_up[l].astype(BF16), w_ffn2_down[l].astype(BF16))
        last = l == depth - 1
        zero_buf = jnp.zeros((bp, CONV_WIDTH - 1, d_rnn), xp.dtype)
        zero_h = jnp.zeros((bp, d_rnn), xp.dtype)
        xp, kp, vp, hp, cp = _layer(xp, zero_buf, zero_h, None, None, lw, lnf, final_norm=last, **cfg)
        xs, kn, vn, hn, cn = _layer(xs, state_lru_conv[l], state_lru_h[l], cache_k_win[l],
                                    cache_v_win[l], lw, lnf, final_norm=last, **cfg)
        for lst, val in zip(outs, (kp, vp, hp, cp, kn, vn, hn, cn)):
            lst.append(val)
    return (xp, xs) + tuple(jnp.stack(lst) for lst in outs)
```

```python
import functools

import jax
import jax.numpy as jnp
import numpy as np
from jax.experimental import pallas as pl
from jax.experimental.pallas import tpu as pltpu

N_HEADS = 8
HEAD_DIM = 64
D_ATTN = N_HEADS * HEAD_DIM
CONV_WIDTH = 4
LRU_C = 8.0
DILATED = ((128, 1), (512, 4), (2048, 16))
MAX_WINDOW = 2048
BLK = 128
EPS = 1e-6

LANES = 128
SUBLANES = 8
VMEM_LIMIT = 56 * 1024 * 1024
VMEM_LIMIT_ATTN = 60 * 1024 * 1024

F32 = jnp.float32
BF16 = jnp.bfloat16
NEG_INF = float("-inf")


def _rms(x, g):
    return x * jax.lax.rsqrt(jnp.mean(x * x, axis=-1, keepdims=True) + EPS) * g


def _swiglu(xn, wg_ref, wu_ref, wd_ref):
    gate = jnp.dot(xn, wg_ref[...], preferred_element_type=F32)
    up = jnp.dot(xn, wu_ref[...], preferred_element_type=F32)
    h = (gate * jax.nn.sigmoid(gate) * up).astype(BF16)
    return jnp.dot(h, wd_ref[...], preferred_element_type=F32)


def _resident(shape):
    return pl.BlockSpec(shape, lambda *_: (0,) * len(shape), pipeline_mode=pl.Buffered(1))


def _ffn_in_body(x_ref, ln1_ref, wg_ref, wu_ref, wd_ref, lnm_ref, win_ref, x1_ref, q_ref, k_ref, v_ref):
    x = x_ref[...]
    xn = _rms(x, ln1_ref[...]).astype(BF16)
    x1 = x + 0.5 * _swiglu(xn, wg_ref, wu_ref, wd_ref)
    x1_ref[...] = x1
    zn = _rms(x1, lnm_ref[...]).astype(BF16)
    z = jnp.dot(zn, win_ref[...], preferred_element_type=F32)
    d = D_ATTN
    q_ref[...] = z[:, :d] * (HEAD_DIM ** -0.5)
    k_ref[...] = z[:, d:2 * d]
    v_ref[...] = z[:, 2 * d:3 * d]
    r = (z.shape[1] - 3 * d) // 2
    return z[:, 3 * d:3 * d + r], z[:, 3 * d + r:]


def _ffn_in_kernel(x_ref, ln1_ref, wg_ref, wu_ref, wd_ref, lnm_ref, win_ref,
                   x1_ref, q_ref, k_ref, v_ref, u_ref, g_ref):
    u, g = _ffn_in_body(x_ref, ln1_ref, wg_ref, wu_ref, wd_ref, lnm_ref, win_ref, x1_ref, q_ref, k_ref, v_ref)
    u_ref[...] = u
    g_ref[...] = g


def _ffn_in_rglru_kernel(x_ref, ln1_ref, wg_ref, wu_ref, wd_ref, lnm_ref, win_ref,
                         cbuf_ref, h0_ref, cw_ref, cb_ref, wa_ref, ba_ref, wx_ref, bx_ref, lam_ref,
                         x1_ref, q_ref, k_ref, v_ref, kt_ref, vt_ref, r_ref, hlast_ref, tail_ref,
                         u_s, g_s, tail_s, hcar, *, tiles_per_seq):
    i = pl.program_id(0)

    @pl.when(i == 0)
    def _():
        for ref in (u_s, g_s, tail_s, hcar):
            ref[...] = jnp.zeros_like(ref)

    starts = (i - 1) % tiles_per_seq == 0
    tail = jnp.where(starts, cbuf_ref[0], tail_s[...])
    h_prev = jnp.where(starts, h0_ref[0], hcar[...])
    r, new_tail, h = _rglru_tile(u_s[...], g_s[...], tail, h_prev, cw_ref, cb_ref, wa_ref, ba_ref,
                                 wx_ref, bx_ref, lam_ref)
    u, g = _ffn_in_body(x_ref, ln1_ref, wg_ref, wu_ref, wd_ref, lnm_ref, win_ref, x1_ref, q_ref, k_ref, v_ref)
    kt_ref[0] = k_ref[...].T
    vt_ref[0] = v_ref[...].T
    tail_s[...] = new_tail
    hcar[...] = h
    r_ref[...] = r
    hlast_ref[0] = h
    tail_ref[0] = new_tail
    u_s[...] = u
    g_s[...] = g


def _ffn_in(x, ln1, wg, wu, wd, lnm, win, *, tm):
    n, dm = x.shape
    dff = wg.shape[1]
    d_rnn = (win.shape[1] - 3 * D_ATTN) // 2
    row = lambda w: pl.BlockSpec((tm, w), lambda i: (i, 0))
    widths = (dm, D_ATTN, D_ATTN, D_ATTN, d_rnn, d_rnn)
    return pl.pallas_call(
        _ffn_in_kernel,
        grid=(n // tm,),
        in_specs=[row(dm), _resident((1, dm)), _resident((dm, dff)), _resident((dm, dff)),
                  _resident((dff, dm)), _resident((1, dm)), _resident(win.shape)],
        out_specs=[row(w) for w in widths],
        out_shape=[jax.ShapeDtypeStruct((n, w), F32) for w in widths],
        compiler_params=pltpu.CompilerParams(dimension_semantics=("arbitrary",),
                                             vmem_limit_bytes=VMEM_LIMIT),
        name="ffn_in",
    )(x, ln1, wg, wu, wd, lnm, win)


def _ffn_in_rglru(x, ln1, wg, wu, wd, lnm, win, conv_buf, h0, conv_w, conv_b, wa_bd, b_a, wx_bd, b_x, lam,
                  *, seq, keep, tm):
    n, dm = x.shape
    dff = wg.shape[1]
    r = conv_w.shape[-1]
    nb = n // seq
    tiles = n // tm
    tps = seq // tm
    first_kept = tps - keep // tm
    pad = SUBLANES
    cbuf = jnp.pad(conv_buf, ((0, 0), (pad - (CONV_WIDTH - 1), 0), (0, 0)))
    cur_tile = lambda i: jnp.minimum(i, tiles - 1)
    cur = lambda w: pl.BlockSpec((tm, w), lambda i: (cur_tile(i), 0))
    kept = pl.BlockSpec((1, D_ATTN, tm), lambda i: (cur_tile(i) // tps, 0,
                                                    jnp.maximum(cur_tile(i) % tps - first_kept, 0)))
    prev = lambda w: pl.BlockSpec((tm, w), lambda i: (jnp.maximum(i - 1, 0), 0))
    prev_seq = lambda rows: pl.BlockSpec((1, rows, r), lambda i: (jnp.maximum(i - 1, 0) // tps, 0, 0))
    vec = _resident((1, r))
    x1, q, k, v, kt, vt, rnn, h_last, tail = pl.pallas_call(
        functools.partial(_ffn_in_rglru_kernel, tiles_per_seq=tps),
        grid=(tiles + 1,),
        in_specs=[cur(dm), _resident((1, dm)), _resident((dm, dff)), _resident((dm, dff)),
                  _resident((dff, dm)), _resident((1, dm)), _resident(win.shape),
                  prev_seq(pad), prev_seq(1), _resident((CONV_WIDTH, r)), vec,
                  _resident((r, r)), vec, _resident((r, r)), vec, vec],
        out_specs=[cur(dm), cur(D_ATTN), cur(D_ATTN), cur(D_ATTN), kept, kept,
                   prev(r), prev_seq(1), prev_seq(pad)],
        out_shape=[jax.ShapeDtypeStruct((n, dm), F32)] + [jax.ShapeDtypeStruct((n, D_ATTN), F32)] * 3
                  + [jax.ShapeDtypeStruct((nb, D_ATTN, keep), F32)] * 2
                  + [jax.ShapeDtypeStruct((n, r), BF16), jax.ShapeDtypeStruct((nb, 1, r), F32),
                     jax.ShapeDtypeStruct((nb, pad, r), F32)],
        scratch_shapes=[pltpu.VMEM((tm, r), F32), pltpu.VMEM((tm, r), F32),
                        pltpu.VMEM((pad, r), F32), pltpu.VMEM((1, r), F32)],
        compiler_params=pltpu.CompilerParams(dimension_semantics=("arbitrary",),
                                             vmem_limit_bytes=VMEM_LIMIT),
        name="ffn_in_rglru",
    )(x, ln1, wg, wu, wd, lnm, win, cbuf, h0.reshape(nb, 1, r), conv_w, conv_b.reshape(1, r),
      wa_bd, b_a.reshape(1, r), wx_bd, b_x.reshape(1, r), lam.reshape(1, r))
    return x1, q, k, v, kt, vt, rnn, h_last.reshape(nb, r), tail[:, pad - (CONV_WIDTH - 1):]


FOLD = 16
BLOCK_UNROLL = 32


def _fold_rows(s):
    return s // FOLD + SUBLANES


def _branch_bias(dil):
    nch = FOLD // dil
    qs = BLK // nch
    ip = np.arange(BLK)
    kp = np.arange(2 * BLK)
    i = nch * (ip % qs) + ip // qs
    kk = nch * (kp % (2 * qs)) + kp // (2 * qs)
    delta = kk[None, :] - i[:, None]
    first = delta <= 0
    other = (delta >= 0) & (delta <= BLK)
    return np.where(np.concatenate([first, other], axis=0), 0.0, NEG_INF).astype(np.float32)


def _attn_prompt_kernel(q_ref, k_ref, v_ref, bias_ref, o_ref, qf, kf, vf, o_acc, m_acc, l_acc):
    s = q_ref.shape[1]
    pr = o_acc.shape[0] // FOLD
    prb = qf.shape[0] // FOLD
    tile = 2 * SUBLANES
    nt = (((1,), (1,)), ((), ()))
    low = jax.lax.broadcasted_iota(jnp.int32, (BLK, LANES), 1) < HEAD_DIM
    log2e = float(np.log2(np.e))

    def fold(mi, c):
        for grp in range(FOLD // SUBLANES):
            src = pl.ds(pl.multiple_of(mi * FOLD + grp * SUBLANES, SUBLANES), SUBLANES)
            dst = pl.ds(mi + grp * SUBLANES * pr, SUBLANES, stride=pr)
            o_acc[dst, :] = q_ref[0, src, :]
            m_acc[dst, :] = k_ref[0, src, :]
            l_acc[dst, :] = v_ref[0, src, :]
        return c

    jax.lax.fori_loop(0, s // FOLD, fold, 0, unroll=8)

    def pack(r, c):
        for ch in range(s // FOLD // BLK):
            src = pl.ds(pl.multiple_of(r * pr + ch * BLK, SUBLANES), BLK)
            dst = pl.ds(pl.multiple_of(r * prb + ch * BLK, tile), BLK)
            qf[dst, :] = (o_acc[src, :] * log2e).astype(BF16)
            kf[dst, :] = m_acc[src, :].astype(BF16)
            vf[dst, :] = l_acc[src, :].astype(BF16)
        return c

    jax.lax.fori_loop(0, FOLD, pack, 0)

    n_branches = len(DILATED)
    for bi, (_, dil) in enumerate(DILATED):
        natural = dil == 1
        nch = 1 if natural else FOLD // dil
        qs = BLK // nch
        nblk = s // (dil * BLK)
        first, last = bi == 0, bi == n_branches - 1
        assert first or not natural

        def block(idx, c, bi=bi, dil=dil, natural=natural, nch=nch, qs=qs, nblk=nblk, first=first, last=last):
            rd = idx // nblk
            j = idx % nblk
            prev = jnp.maximum(j - 1, 0)

            def gather(ref, base, n):
                if natural:
                    return ref[0, pl.ds(pl.multiple_of(base, BLK), n), :]
                return jnp.concatenate(
                    [ref[pl.ds(pl.multiple_of(rd * prb + base + a * dil * prb, tile), n), :] for a in range(nch)],
                    axis=0)

            if natural:
                q2 = (gather(q_ref, qs * j, qs) * log2e).astype(BF16)
                k2 = gather(k_ref, qs * prev, 2 * qs).astype(BF16)
                v2 = gather(v_ref, qs * prev, 2 * qs).astype(BF16)
            else:
                q2 = gather(qf, qs * j, qs)
                k2 = gather(kf, qs * prev, 2 * qs)
                v2 = gather(vf, qs * prev, 2 * qs)
            brow = (2 * bi + jnp.minimum(j, 1)) * BLK
            bias = bias_ref[pl.ds(pl.multiple_of(brow, BLK), BLK), :]
            res = []
            for sel in (low, ~low):
                qm = jnp.where(sel, q2, jnp.zeros_like(q2))
                sc = jax.lax.dot_general(qm, k2, nt, preferred_element_type=F32) + bias
                m = jnp.max(sc, axis=-1, keepdims=True)
                p = jnp.exp2(sc - m)
                l = jnp.sum(p, axis=-1, keepdims=True)
                o = jnp.dot(p.astype(BF16), v2, preferred_element_type=F32)
                res.append((o, m, l))
            o_b, m_b, l_b = (jnp.where(low, x0, x1) for x0, x1 in zip(*res))
            if natural:
                for g in range(BLK // SUBLANES):
                    rows = pl.ds((g % 2) * SUBLANES * pr + (BLK // FOLD) * j + g // 2, SUBLANES, stride=pr)
                    part = slice(g * SUBLANES, (g + 1) * SUBLANES)
                    o_acc[rows, :] = o_b[part]
                    m_acc[rows, :] = m_b[part]
                    l_acc[rows, :] = l_b[part]
                return c
            for a in range(nch):
                rows = pl.ds(pl.multiple_of(rd * pr + qs * j + a * dil * pr, SUBLANES), qs)
                part = slice(a * qs, (a + 1) * qs)
                m_old = m_acc[rows, :]
                m_new = jnp.maximum(m_old, m_b[part])
                keep = m_old >= m_b[part]
                e_min = jnp.exp2(jnp.minimum(m_old, m_b[part]) - m_new)
                e_old = jnp.where(keep, 1.0, e_min)
                e_b = jnp.where(keep, e_min, 1.0)
                o_new = o_acc[rows, :] * e_old + o_b[part] * e_b
                l_new = l_acc[rows, :] * e_old + l_b[part] * e_b
                if last:
                    o_acc[rows, :] = o_new * (1.0 / l_new)
                else:
                    o_acc[rows, :] = o_new
                    m_acc[rows, :] = m_new
                    l_acc[rows, :] = l_new
            return c

        jax.lax.fori_loop(0, dil * nblk, block, 0, unroll=BLOCK_UNROLL)

    def unfold(mi, c):
        rows = o_acc[pl.ds(mi, FOLD, stride=pr), :]
        o_ref[0, pl.ds(pl.multiple_of(mi * FOLD, FOLD), FOLD), :] = rows.astype(BF16)
        return c

    jax.lax.fori_loop(0, s // FOLD, unfold, 0, unroll=4)


def _attn_prompt(q, k, v):
    b, s, d = q.shape
    assert s % (FOLD * BLK) == 0 and s // FOLD >= 2 * BLK and all(FOLD % dil == 0 for _, dil in DILATED)
    assert DILATED[0][1] == 1 and FOLD == 2 * SUBLANES
    bias = np.concatenate([_branch_bias(FOLD if dil == 1 else dil) for _, dil in DILATED], axis=0)
    seq = pl.BlockSpec((1, s, LANES), lambda bi, hp: (bi, 0, hp))
    folded = pltpu.VMEM((FOLD * (s // FOLD + 2 * SUBLANES), LANES), BF16)
    acc = pltpu.VMEM((FOLD * _fold_rows(s), LANES), F32)
    return pl.pallas_call(
        _attn_prompt_kernel,
        grid=(b, d // LANES),
        in_specs=[seq, seq, seq, _resident(bias.shape)],
        out_specs=pl.BlockSpec((1, s, LANES), lambda bi, hp: (bi, 0, hp)),
        out_shape=jax.ShapeDtypeStruct((b, s, d), BF16),
        scratch_shapes=[folded] * 3 + [acc] * 3,
        compiler_params=pltpu.CompilerParams(dimension_semantics=("arbitrary", "arbitrary"),
                                             vmem_limit_bytes=VMEM_LIMIT_ATTN),
        name="attn_prompt",
    )(q, k, v, jnp.asarray(bias))


def _branch_counts(t, w_buf):
    i = np.arange(t)[:, None]
    c = np.arange(w_buf)[None, :]
    n = np.arange(LANES)[None, :]
    cnt_c = np.zeros((t, w_buf), np.float32)
    cnt_n = np.zeros((t, LANES), np.float32)
    for win, dil in DILATED:
        dist = w_buf + i - c
        cnt_c += ((dist % dil == 0) & (dist >= dil) & (dist <= win)).astype(np.float32)
        dist = i - n
        cnt_n += ((n < t) & (dist >= 0) & (dist % dil == 0) & (dist <= win)).astype(np.float32)
    return np.concatenate([cnt_c, cnt_n], axis=1)


def _attn_sample_kernel(q_ref, knt_ref, vnt_ref, kc_ref, vc_ref, cnt_ref, ot_ref, knp, vnp, *, t):
    rows = 2 * SUBLANES

    @pl.when(pl.program_id(0) == 0)
    def _():
        knp[...] = jnp.zeros_like(knp)
        vnp[...] = jnp.zeros_like(vnp)

    knp[:, 0:t] = knt_ref[0]
    vnp[:, 0:t] = vnt_ref[0]
    cnt = cnt_ref[...]
    valid = cnt > 0.0
    q = jnp.concatenate([q_ref[0], jnp.zeros((rows - t, D_ATTN), F32)], axis=0).astype(BF16)
    ones = jnp.ones((SUBLANES, cnt.shape[1]), F32)
    nt = (((1,), (1,)), ((), ()))
    for h in range(N_HEADS):
        hrows = slice(h * HEAD_DIM, (h + 1) * HEAD_DIM)
        kt = jnp.concatenate([kc_ref[0, h], knp[hrows, :]], axis=1).astype(BF16)
        sc = jnp.dot(q[:, hrows], kt, preferred_element_type=F32)
        sc = jnp.where(valid, sc, NEG_INF)
        m = jnp.maximum(jnp.max(sc, axis=-1, keepdims=True), -1e30)
        p = (cnt * jnp.exp(sc - m)).astype(BF16)
        vt = jnp.concatenate([vc_ref[0, h], vnp[hrows, :]], axis=1)
        vt = jnp.concatenate([vt, ones], axis=0).astype(BF16)
        o = jax.lax.dot_general(vt, p, nt, preferred_element_type=F32)
        ot_ref[0, hrows, :] = (o[0:HEAD_DIM] * (1.0 / o[HEAD_DIM:HEAD_DIM + 1]))[:, 0:t]


def _attn_sample(q, k_new, v_new, k_cache_t, v_cache_t):
    b, t, d = q.shape
    w = k_cache_t.shape[3]
    assert t <= SUBLANES and w % LANES == 0
    cnt = np.zeros((2 * SUBLANES, w + LANES), np.float32)
    cnt[:t] = _branch_counts(t, w)
    tr = lambda a: jnp.swapaxes(a, 1, 2)
    tok = pl.BlockSpec((1, t, d), lambda bi: (bi, 0, 0))
    tok_t = pl.BlockSpec((1, d, t), lambda bi: (bi, 0, 0))
    cache = pl.BlockSpec((1,) + k_cache_t.shape[1:], lambda bi: (bi, 0, 0, 0))
    out = pl.pallas_call(
        functools.partial(_attn_sample_kernel, t=t),
        grid=(b,),
        in_specs=[tok, tok_t, tok_t, cache, cache, _resident(cnt.shape)],
        out_specs=tok_t,
        out_shape=jax.ShapeDtypeStruct((b, d, t), F32),
        scratch_shapes=[pltpu.VMEM((d, LANES), F32), pltpu.VMEM((d, LANES), F32)],
        compiler_params=pltpu.CompilerParams(dimension_semantics=("arbitrary",),
                                             vmem_limit_bytes=VMEM_LIMIT),
        name="attn_sample",
    )(q, tr(k_new), tr(v_new), k_cache_t, v_cache_t, jnp.asarray(cnt))
    return tr(out)


def _rglru_coeffs(xc, wa_ref, ba_ref, wx_ref, bx_ref, lam_ref):
    xb = xc.astype(BF16)
    rg = jax.nn.sigmoid(jnp.dot(xb, wa_ref[...], preferred_element_type=F32) + ba_ref[...])
    ig = jax.nn.sigmoid(jnp.dot(xb, wx_ref[...], preferred_element_type=F32) + bx_ref[...])
    log_a = (-LRU_C * jax.nn.softplus(-lam_ref[...])) * rg
    a = jnp.exp(log_a)
    return a, jnp.sqrt(-jnp.tanh(log_a) * (a * a + 1.0)) * ig * xc


def _rglru_tile(u, g, tail, h_prev, cw_ref, cb_ref, wa_ref, ba_ref, wx_ref, bx_ref, lam_ref):
    tc = u.shape[0]
    pad = SUBLANES
    full = jnp.concatenate([tail, u], axis=0)
    first = pad - (CONV_WIDTH - 1)
    xc = cb_ref[...] + cw_ref[0:1, :] * full[first:first + tc]
    for j in range(1, CONV_WIDTH):
        xc = xc + cw_ref[j:j + 1, :] * full[first + j:first + j + tc]
    new_tail = full[tc:tc + pad]
    a, b = _rglru_coeffs(xc, wa_ref, ba_ref, wx_ref, bx_ref, lam_ref)

    row = jax.lax.broadcasted_iota(jnp.int32, (SUBLANES, u.shape[1]), 0)
    h = h_prev
    hs = []
    for gi in range(tc // SUBLANES):
        ag = a[gi * SUBLANES:(gi + 1) * SUBLANES]
        bg = b[gi * SUBLANES:(gi + 1) * SUBLANES]
        for sh in (1, 2, 4):
            a_prev = pltpu.roll(ag, sh, axis=0)
            b_prev = pltpu.roll(bg, sh, axis=0)
            take = row >= sh
            bg = jnp.where(take, ag * b_prev + bg, bg)
            ag = jnp.where(take, ag * a_prev, ag)
        hg = ag * h + bg
        hs.append(hg)
        h = hg[SUBLANES - 1:SUBLANES, :]
    r = (jnp.concatenate(hs, axis=0) * jax.nn.gelu(g)).astype(BF16)
    return r, new_tail, h


def _rglru_steps_kernel(u_ref, g_ref, cbuf_ref, h0_ref, cw_ref, cb_ref, wa_ref, ba_ref, wx_ref, bx_ref,
                        lam_ref, r_ref, hlast_ref, tail_ref, *, t):
    nb = h0_ref.shape[0]
    taps = CONV_WIDTH - 1
    step_rows = lambda ti: pl.ds(ti, nb, stride=t)
    full = [cbuf_ref[j] for j in range(taps)] + [u_ref[step_rows(ti), :] for ti in range(t)]
    xcs = []
    for ti in range(t):
        xc = cb_ref[...] + cw_ref[0:1, :] * full[ti]
        for j in range(1, CONV_WIDTH):
            xc = xc + cw_ref[j:j + 1, :] * full[ti + j]
        xcs.append(xc)
    a, b = _rglru_coeffs(jnp.concatenate(xcs, axis=0), wa_ref, ba_ref, wx_ref, bx_ref, lam_ref)
    h = h0_ref[...]
    for ti in range(t):
        rows = slice(ti * nb, (ti + 1) * nb)
        h = a[rows] * h + b[rows]
        r_ref[step_rows(ti), :] = h * jax.nn.gelu(g_ref[step_rows(ti), :])
    hlast_ref[...] = h
    for j in range(taps):
        tail_ref[j] = full[t + j]


def _rglru_steps(u, g, conv_buf, h0, conv_w, conv_b, wa_bd, b_a, wx_bd, b_x, lam, *, t):
    n, r = u.shape
    nb = n // t
    taps = CONV_WIDTH - 1
    assert t >= taps
    col = lambda rows: pl.BlockSpec((rows, LANES), lambda c: (0, c))
    state = pl.BlockSpec((taps, nb, LANES), lambda c: (0, 0, c))
    diag = pl.BlockSpec((LANES, LANES), lambda c: (c, c))
    out, h_last, tail = pl.pallas_call(
        functools.partial(_rglru_steps_kernel, t=t),
        grid=(r // LANES,),
        in_specs=[col(n), col(n), state, col(nb), col(CONV_WIDTH), col(1), diag, col(1), diag, col(1), col(1)],
        out_specs=[col(n), col(nb), state],
        out_shape=[jax.ShapeDtypeStruct((n, r), F32), jax.ShapeDtypeStruct((nb, r), F32),
                   jax.ShapeDtypeStruct((taps, nb, r), F32)],
        compiler_params=pltpu.CompilerParams(dimension_semantics=("arbitrary",),
                                             vmem_limit_bytes=VMEM_LIMIT),
        name="rglru_steps",
    )(u, g, jnp.swapaxes(conv_buf, 0, 1), h0, conv_w, conv_b.reshape(1, r), wa_bd, b_a.reshape(1, r),
      wx_bd, b_x.reshape(1, r), lam.reshape(1, r))
    return out, h_last, jnp.swapaxes(tail, 0, 1)


def _out_ffn_kernel(x1_ref, attn_ref, r_ref, wo_ref, ln2_ref, wg_ref, wu_ref, wd_ref, lnf_ref, y_ref,
                    *, final_norm):
    mix = jnp.concatenate([attn_ref[...].astype(BF16), r_ref[...].astype(BF16)], axis=-1)
    x2 = x1_ref[...] + jnp.dot(mix, wo_ref[...], preferred_element_type=F32)
    xn = _rms(x2, ln2_ref[...]).astype(BF16)
    x3 = x2 + 0.5 * _swiglu(xn, wg_ref, wu_ref, wd_ref)
    y_ref[...] = _rms(x3, lnf_ref[...]) if final_norm else x3


def _out_ffn(x1, attn, r, wo, ln2, wg, wu, wd, lnf, *, final_norm, tm):
    n, dm = x1.shape
    dff = wg.shape[1]
    row = lambda w: pl.BlockSpec((tm, w), lambda i: (i, 0))
    return pl.pallas_call(
        functools.partial(_out_ffn_kernel, final_norm=final_norm),
        grid=(n // tm,),
        in_specs=[row(dm), row(D_ATTN), row(r.shape[1]), _resident(wo.shape), _resident((1, dm)),
                  _resident((dm, dff)), _resident((dm, dff)), _resident((dff, dm)), _resident((1, dm))],
        out_specs=row(dm),
        out_shape=jax.ShapeDtypeStruct((n, dm), F32),
        compiler_params=pltpu.CompilerParams(dimension_semantics=("arbitrary",),
                                             vmem_limit_bytes=VMEM_LIMIT),
        name="out_ffn",
    )(x1, attn, r, wo, ln2, wg, wu, wd, lnf)


def _block_diag(w):
    nb, c, d = w.shape
    return jnp.einsum("ncd,nm->ncmd", w, jnp.eye(nb, dtype=w.dtype)).reshape(nb * c, nb * d)


def _layer(x, conv_buf, h0, k_past, v_past, lw, lnf, *, final_norm, tm, tm_out):
    (ln1, w1g, w1u, w1d, ln_m, w_in, conv_w, conv_b, wa_bd, b_a, wx_bd, b_x, lam,
     w_out, ln2, w2g, w2u, w2d) = lw
    b, t, dm = x.shape
    r = conv_w.shape[-1]
    prompt = k_past is None
    keep = min(MAX_WINDOW, t) if prompt else t
    shp = lambda a: a.reshape(b, t, a.shape[-1])
    rnn_w = (conv_w, conv_b, wa_bd, b_a, wx_bd, b_x, lam)
    if prompt:
        x1, q, k, v, kt, vt, rnn, h_last, new_buf = _ffn_in_rglru(
            x.reshape(b * t, dm), ln1, w1g, w1u, w1d, ln_m, w_in, conv_buf, h0, *rnn_w, seq=t, keep=keep, tm=tm)
        attn = _attn_prompt(shp(q), shp(k), shp(v))
        k_state = jnp.transpose(kt.reshape(b, N_HEADS, HEAD_DIM, keep), (0, 3, 1, 2))
        v_state = jnp.transpose(vt.reshape(b, N_HEADS, HEAD_DIM, keep), (0, 3, 1, 2))
    else:
        x1, q, k, v, u, g = _ffn_in(x.reshape(b * t, dm), ln1, w1g, w1u, w1d, ln_m, w_in, tm=tm)
        attn = _attn_sample(shp(q), shp(k), shp(v),
                            jnp.transpose(k_past, (0, 2, 3, 1)), jnp.transpose(v_past, (0, 2, 3, 1)))
        rnn, h_last, new_buf = _rglru_steps(u, g, conv_buf, h0, *rnn_w, t=t)
        k_state = shp(k).reshape(b, t, N_HEADS, HEAD_DIM)
        v_state = shp(v).reshape(b, t, N_HEADS, HEAD_DIM)
    y = _out_ffn(x1, attn.reshape(b * t, D_ATTN), rnn.reshape(b * t, r), w_out, ln2, w2g, w2u, w2d, lnf,
                 final_norm=final_norm, tm=tm_out)
    return y.reshape(b, t, dm), k_state, v_state, h_last, new_buf


def kernel(x_prompt, x_sample, cache_k_win, cache_v_win, state_lru_h, state_lru_conv, ln_ffn1, w_ffn1_gate, w_ffn1_up, w_ffn1_down, ln_mix, w_in, conv_w, conv_b, w_gate_a, b_gate_a, w_gate_x, b_gate_x, lru_lambda, w_out, ln_ffn2, w_ffn2_gate, w_ffn2_up, w_ffn2_down, ln_final):
    depth = ln_ffn1.shape[0]
    dm = x_prompt.shape[-1]
    d_rnn = conv_w.shape[-1]
    bp = x_prompt.shape[0]
    xp, xs = x_prompt, x_sample
    outs = [[] for _ in range(8)]
    lnf = ln_final.reshape(1, dm)
    cfg = dict(tm=256, tm_out=512)
    for l in range(depth):
        lw = (ln_ffn1[l].reshape(1, dm), w_ffn1_gate[l].astype(BF16), w_ffn1_up[l].astype(BF16),
              w_ffn1_down[l].astype(BF16), ln_mix[l].reshape(1, dm), w_in[l].astype(BF16),
              conv_w[l], conv_b[l], _block_diag(w_gate_a[l]).astype(BF16), b_gate_a[l].reshape(-1),
              _block_diag(w_gate_x[l]).astype(BF16), b_gate_x[l].reshape(-1), lru_lambda[l],
              w_out[l].astype(BF16), ln_ffn2[l].reshape(1, dm), w_ffn2_gate[l].astype(BF16),
              w_ffn2_up[l].astype(BF16), w_ffn2_down[l].astype(BF16))
        last = l == depth - 1
        zero_buf = jnp.zeros((bp, CONV_WIDTH - 1, d_rnn), xp.dtype)
        zero_h = jnp.zeros((bp, d_rnn), xp.dtype)
        xp, kp, vp, hp, cp = _layer(xp, zero_buf, zero_h, None, None, lw, lnf, final_norm=last, **cfg)
        xs, kn, vn, hn, cn = _layer(xs, state_lru_conv[l], state_lru_h[l], cache_k_win[l],
                                    cache_v_win[l], lw, lnf, final_norm=last, **cfg)
        for lst, val in zip(outs, (kp, vp, hp, cp, kn, vn, hn, cn)):
            lst.append(val)
    return (xp, xs) + tuple(jnp.stack(lst) for lst in outs)
```

```python
import functools

import jax
import jax.numpy as jnp
import numpy as np
from jax.experimental import pallas as pl
from jax.experimental.pallas import tpu as pltpu

N_HEADS = 8
HEAD_DIM = 64
D_ATTN = N_HEADS * HEAD_DIM
CONV_WIDTH = 4
LRU_C = 8.0
DILATED = ((128, 1), (512, 4), (2048, 16))
MAX_WINDOW = 2048
BLK = 128
EPS = 1e-6

LANES = 128
SUBLANES = 8
VMEM_LIMIT = 56 * 1024 * 1024
VMEM_LIMIT_ATTN = 60 * 1024 * 1024

F32 = jnp.float32
BF16 = jnp.bfloat16
NEG_INF = float("-inf")


def _rms(x, g):
    return x * jax.lax.rsqrt(jnp.mean(x * x, axis=-1, keepdims=True) + EPS) * g


def _swiglu(xn, wg_ref, wu_ref, wd_ref):
    gate = jnp.dot(xn, wg_ref[...], preferred_element_type=F32)
    up = jnp.dot(xn, wu_ref[...], preferred_element_type=F32)
    h = (gate * jax.nn.sigmoid(gate) * up).astype(BF16)
    return jnp.dot(h, wd_ref[...], preferred_element_type=F32)


def _resident(shape):
    return pl.BlockSpec(shape, lambda *_: (0,) * len(shape), pipeline_mode=pl.Buffered(1))


def _ffn_in_body(x_ref, ln1_ref, wg_ref, wu_ref, wd_ref, lnm_ref, win_ref, x1_ref, q_ref, k_ref, v_ref):
    x = x_ref[...]
    xn = _rms(x, ln1_ref[...]).astype(BF16)
    x1 = x + 0.5 * _swiglu(xn, wg_ref, wu_ref, wd_ref)
    x1_ref[...] = x1
    zn = _rms(x1, lnm_ref[...]).astype(BF16)
    z = jnp.dot(zn, win_ref[...], preferred_element_type=F32)
    d = D_ATTN
    q_ref[...] = z[:, :d] * (HEAD_DIM ** -0.5)
    k_ref[...] = z[:, d:2 * d]
    v_ref[...] = z[:, 2 * d:3 * d]
    r = (z.shape[1] - 3 * d) // 2
    return z[:, 3 * d:3 * d + r], z[:, 3 * d + r:]


def _ffn_in_kernel(x_ref, ln1_ref, wg_ref, wu_ref, wd_ref, lnm_ref, win_ref,
                   x1_ref, q_ref, k_ref, v_ref, u_ref, g_ref):
    u, g = _ffn_in_body(x_ref, ln1_ref, wg_ref, wu_ref, wd_ref, lnm_ref, win_ref, x1_ref, q_ref, k_ref, v_ref)
    u_ref[...] = u
    g_ref[...] = g


def _ffn_in_rglru_kernel(x_ref, ln1_ref, wg_ref, wu_ref, wd_ref, lnm_ref, win_ref,
                         cbuf_ref, h0_ref, cw_ref, cb_ref, wa_ref, ba_ref, wx_ref, bx_ref, lam_ref,
                         x1_ref, q_ref, k_ref, v_ref, kt_ref, vt_ref, r_ref, hlast_ref, tail_ref,
                         u_s, g_s, tail_s, hcar, *, tiles_per_seq):
    i = pl.program_id(0)

    @pl.when(i == 0)
    def _():
        for ref in (u_s, g_s, tail_s, hcar):
            ref[...] = jnp.zeros_like(ref)

    starts = (i - 1) % tiles_per_seq == 0
    tail = jnp.where(starts, cbuf_ref[0], tail_s[...])
    h_prev = jnp.where(starts, h0_ref[0], hcar[...])
    r, new_tail, h = _rglru_tile(u_s[...], g_s[...], tail, h_prev, cw_ref, cb_ref, wa_ref, ba_ref,
                                 wx_ref, bx_ref, lam_ref)
    u, g = _ffn_in_body(x_ref, ln1_ref, wg_ref, wu_ref, wd_ref, lnm_ref, win_ref, x1_ref, q_ref, k_ref, v_ref)
    kt_ref[0] = k_ref[...].T
    vt_ref[0] = v_ref[...].T
    tail_s[...] = new_tail
    hcar[...] = h
    r_ref[...] = r
    hlast_ref[0] = h
    tail_ref[0] = new_tail
    u_s[...] = u
    g_s[...] = g


def _ffn_in(x, ln1, wg, wu, wd, lnm, win, *, tm):
    n, dm = x.shape
    dff = wg.shape[1]
    d_rnn = (win.shape[1] - 3 * D_ATTN) // 2
    row = lambda w: pl.BlockSpec((tm, w), lambda i: (i, 0))
    widths = (dm, D_ATTN, D_ATTN, D_ATTN, d_rnn, d_rnn)
    return pl.pallas_call(
        _ffn_in_kernel,
        grid=(n // tm,),
        in_specs=[row(dm), _resident((1, dm)), _resident((dm, dff)), _resident((dm, dff)),
                  _resident((dff, dm)), _resident((1, dm)), _resident(win.shape)],
        out_specs=[row(w) for w in widths],
        out_shape=[jax.ShapeDtypeStruct((n, w), F32) for w in widths],
        compiler_params=pltpu.CompilerParams(dimension_semantics=("arbitrary",),
                                             vmem_limit_bytes=VMEM_LIMIT),
        name="ffn_in",
    )(x, ln1, wg, wu, wd, lnm, win)


def _ffn_in_rglru(x, ln1, wg, wu, wd, lnm, win, conv_buf, h0, conv_w, conv_b, wa_bd, b_a, wx_bd, b_x, lam,
                  *, seq, keep, tm):
    n, dm = x.shape
    dff = wg.shape[1]
    r = conv_w.shape[-1]
    nb = n // seq
    tiles = n // tm
    tps = seq // tm
    first_kept = tps - keep // tm
    pad = SUBLANES
    cbuf = jnp.pad(conv_buf, ((0, 0), (pad - (CONV_WIDTH - 1), 0), (0, 0)))
    cur_tile = lambda i: jnp.minimum(i, tiles - 1)
    cur = lambda w: pl.BlockSpec((tm, w), lambda i: (cur_tile(i), 0))
    kept = pl.BlockSpec((1, D_ATTN, tm), lambda i: (cur_tile(i) // tps, 0,
                                                    jnp.maximum(cur_tile(i) % tps - first_kept, 0)))
    prev = lambda w: pl.BlockSpec((tm, w), lambda i: (jnp.maximum(i - 1, 0), 0))
    prev_seq = lambda rows: pl.BlockSpec((1, rows, r), lambda i: (jnp.maximum(i - 1, 0) // tps, 0, 0))
    vec = _resident((1, r))
    x1, q, k, v, kt, vt, rnn, h_last, tail = pl.pallas_call(
        functools.partial(_ffn_in_rglru_kernel, tiles_per_seq=tps),
        grid=(tiles + 1,),
        in_specs=[cur(dm), _resident((1, dm)), _resident((dm, dff)), _resident((dm, dff)),
                  _resident((dff, dm)), _resident((1, dm)), _resident(win.shape),
                  prev_seq(pad), prev_seq(1), _resident((CONV_WIDTH, r)), vec,
                  _resident((r, r)), vec, _resident((r, r)), vec, vec],
        out_specs=[cur(dm), cur(D_ATTN), cur(D_ATTN), cur(D_ATTN), kept, kept,
                   prev(r), prev_seq(1), prev_seq(pad)],
        out_shape=[jax.ShapeDtypeStruct((n, dm), F32)] + [jax.ShapeDtypeStruct((n, D_ATTN), F32)] * 3
                  + [jax.ShapeDtypeStruct((nb, D_ATTN, keep), F32)] * 2
                  + [jax.ShapeDtypeStruct((n, r), BF16), jax.ShapeDtypeStruct((nb, 1, r), F32),
                     jax.ShapeDtypeStruct((nb, pad, r), F32)],
        scratch_shapes=[pltpu.VMEM((tm, r), F32), pltpu.VMEM((tm, r), F32),
                        pltpu.VMEM((pad, r), F32), pltpu.VMEM((1, r), F32)],
        compiler_params=pltpu.CompilerParams(dimension_semantics=("arbitrary",),
                                             vmem_limit_bytes=VMEM_LIMIT),
        name="ffn_in_rglru",
    )(x, ln1, wg, wu, wd, lnm, win, cbuf, h0.reshape(nb, 1, r), conv_w, conv_b.reshape(1, r),
      wa_bd, b_a.reshape(1, r), wx_bd, b_x.reshape(1, r), lam.reshape(1, r))
    return x1, q, k, v, kt, vt, rnn, h_last.reshape(nb, r), tail[:, pad - (CONV_WIDTH - 1):]


FOLD = 16
BLOCK_UNROLL = 32


def _fold_rows(s):
    return s // FOLD + SUBLANES


def _branch_bias(dil):
    nch = FOLD // dil
    qs = BLK // nch
    ip = np.arange(BLK)
    kp = np.arange(2 * BLK)
    i = nch * (ip % qs) + ip // qs
    kk = nch * (kp % (2 * qs)) + kp // (2 * qs)
    delta = kk[None, :] - i[:, None]
    first = delta <= 0
    other = (delta >= 0) & (delta <= BLK)
    return np.where(np.concatenate([first, other], axis=0), 0.0, NEG_INF).astype(np.float32)


def _attn_prompt_kernel(q_ref, k_ref, v_ref, bias_ref, o_ref, qf, kf, vf, o_acc, m_acc, l_acc):
    s = q_ref.shape[1]
    pr = o_acc.shape[0] // FOLD
    prb = qf.shape[0] // FOLD
    tile = 2 * SUBLANES
    nt = (((1,), (1,)), ((), ()))
    low = jax.lax.broadcasted_iota(jnp.int32, (BLK, LANES), 1) < HEAD_DIM
    log2e = float(np.log2(np.e))

    def fold(mi, c):
        for grp in range(FOLD // SUBLANES):
            src = pl.ds(pl.multiple_of(mi * FOLD + grp * SUBLANES, SUBLANES), SUBLANES)
            dst = pl.ds(mi + grp * SUBLANES * pr, SUBLANES, stride=pr)
            o_acc[dst, :] = q_ref[0, src, :]
            m_acc[dst, :] = k_ref[0, src, :]
            l_acc[dst, :] = v_ref[0, src, :]
        return c

    jax.lax.fori_loop(0, s // FOLD, fold, 0, unroll=8)

    def pack(r, c):
        for ch in range(s // FOLD // BLK):
            src = pl.ds(pl.multiple_of(r * pr + ch * BLK, SUBLANES), BLK)
            dst = pl.ds(pl.multiple_of(r * prb + ch * BLK, tile), BLK)
            qf[dst, :] = (o_acc[src, :] * log2e).astype(BF16)
            kf[dst, :] = m_acc[src, :].astype(BF16)
            vf[dst, :] = l_acc[src, :].astype(BF16)
        return c

    jax.lax.fori_loop(0, FOLD, pack, 0)

    n_branches = len(DILATED)
    for bi, (_, dil) in enumerate(DILATED):
        natural = dil == 1
        nch = 1 if natural else FOLD // dil
        qs = BLK // nch
        nblk = s // (dil * BLK)
        first, last = bi == 0, bi == n_branches - 1
        assert first or not natural

        def block(idx, c, bi=bi, dil=dil, natural=natural, nch=nch, qs=qs, nblk=nblk, first=first, last=last):
            rd = idx // nblk
            j = idx % nblk
            prev = jnp.maximum(j - 1, 0)

            def gather(ref, base, n):
                if natural:
                    return ref[0, pl.ds(pl.multiple_of(base, BLK), n), :]
                return jnp.concatenate(
                    [ref[pl.ds(pl.multiple_of(rd * prb + base + a * dil * prb, tile), n), :] for a in range(nch)],
                    axis=0)

            if natural:
                q2 = (gather(q_ref, qs * j, qs) * log2e).astype(BF16)
                k2 = gather(k_ref, qs * prev, 2 * qs).astype(BF16)
                v2 = gather(v_ref, qs * prev, 2 * qs).astype(BF16)
            else:
                q2 = gather(qf, qs * j, qs)
                k2 = gather(kf, qs * prev, 2 * qs)
                v2 = gather(vf, qs * prev, 2 * qs)
            brow = (2 * bi + jnp.minimum(j, 1)) * BLK
            bias = bias_ref[pl.ds(pl.multiple_of(brow, BLK), BLK), :]
            res = []
            for sel in (low, ~low):
                qm = jnp.where(sel, q2, jnp.zeros_like(q2))
                sc = jax.lax.dot_general(qm, k2, nt, preferred_element_type=F32) + bias
                m = jnp.max(sc, axis=-1, keepdims=True)
                p = jnp.exp2(sc - m)
                l = jnp.sum(p, axis=-1, keepdims=True)
                o = jnp.dot(p.astype(BF16), v2, preferred_element_type=F32)
                res.append((o, m, l))
            o_b, m_b, l_b = (jnp.where(low, x0, x1) for x0, x1 in zip(*res))
            if natural:
                for g in range(BLK // SUBLANES):
                    rows = pl.ds((g % 2) * SUBLANES * pr + (BLK // FOLD) * j + g // 2, SUBLANES, stride=pr)
                    part = slice(g * SUBLANES, (g + 1) * SUBLANES)
                    o_acc[rows, :] = o_b[part]
                    m_acc[rows, :] = m_b[part]
                    l_acc[rows, :] = l_b[part]
                return c
            for a in range(nch):
                rows = pl.ds(pl.multiple_of(rd * pr + qs * j + a * dil * pr, SUBLANES), qs)
                part = slice(a * qs, (a + 1) * qs)
                m_old = m_acc[rows, :]
                m_new = jnp.maximum(m_old, m_b[part])
                keep = m_old >= m_b[part]
                e_min = jnp.exp2(jnp.minimum(m_old, m_b[part]) - m_new)
                e_old = jnp.where(keep, 1.0, e_min)
                e_b = jnp.where(keep, e_min, 1.0)
                o_new = o_acc[rows, :] * e_old + o_b[part] * e_b
                l_new = l_acc[rows, :] * e_old + l_b[part] * e_b
                if last:
                    o_acc[rows, :] = o_new * (1.0 / l_new)
                else:
                    o_acc[rows, :] = o_new
                    m_acc[rows, :] = m_new
                    l_acc[rows, :] = l_new
            return c

        jax.lax.fori_loop(0, dil * nblk, block, 0, unroll=BLOCK_UNROLL)

    def unfold(mi, c):
        rows = o_acc[pl.ds(mi, FOLD, stride=pr), :]
        o_ref[0, pl.ds(pl.multiple_of(mi * FOLD, FOLD), FOLD), :] = rows.astype(BF16)
        return c

    jax.lax.fori_loop(0, s // FOLD, unfold, 0, unroll=4)


def _attn_prompt(q, k, v):
    b, s, d = q.shape
    assert s % (FOLD * BLK) == 0 and s // FOLD >= 2 * BLK and all(FOLD % dil == 0 for _, dil in DILATED)
    assert DILATED[0][1] == 1 and FOLD == 2 * SUBLANES
    bias = np.concatenate([_branch_bias(FOLD if dil == 1 else dil) for _, dil in DILATED], axis=0)
    seq = pl.BlockSpec((1, s, LANES), lambda bi, hp: (bi, 0, hp))
    folded = pltpu.VMEM((FOLD * (s // FOLD + 2 * SUBLANES), LANES), BF16)
    acc = pltpu.VMEM((FOLD * _fold_rows(s), LANES), F32)
    return pl.pallas_call(
        _attn_prompt_kernel,
        grid=(b, d // LANES),
        in_specs=[seq, seq, seq, _resident(bias.shape)],
        out_specs=pl.BlockSpec((1, s, LANES), lambda bi, hp: (bi, 0, hp)),
        out_shape=jax.ShapeDtypeStruct((b, s, d), BF16),
        scratch_shapes=[folded] * 3 + [acc] * 3,
        compiler_params=pltpu.CompilerParams(dimension_semantics=("arbitrary", "arbitrary"),
                                             vmem_limit_bytes=VMEM_LIMIT_ATTN),
        name="attn_prompt",
    )(q, k, v, jnp.asarray(bias))


def _branch_counts(t, w_buf):
    i = np.arange(t)[:, None]
    c = np.arange(w_buf)[None, :]
    n = np.arange(LANES)[None, :]
    cnt_c = np.zeros((t, w_buf), np.float32)
    cnt_n = np.zeros((t, LANES), np.float32)
    for win, dil in DILATED:
        dist = w_buf + i - c
        cnt_c += ((dist % dil == 0) & (dist >= dil) & (dist <= win)).astype(np.float32)
        dist = i - n
        cnt_n += ((n < t) & (dist >= 0) & (dist % dil == 0) & (dist <= win)).astype(np.float32)
    return np.concatenate([cnt_c, cnt_n], axis=1)


def _attn_sample_kernel(q_ref, kn_ref, vn_ref, kc_ref, vc_ref, cnt_ref, o_ref, knp, vnp, *, t):
    rows = 2 * SUBLANES

    @pl.when(pl.program_id(0) == 0)
    def _():
        knp[...] = jnp.zeros_like(knp)
        vnp[...] = jnp.zeros_like(vnp)

    knp[:, 0:t] = kn_ref[0].T
    vnp[:, 0:t] = vn_ref[0].T
    cnt = cnt_ref[...]
    valid = cnt > 0.0
    q = jnp.concatenate([q_ref[0], jnp.zeros((rows - t, D_ATTN), F32)], axis=0).astype(BF16)
    ones = jnp.ones((SUBLANES, cnt.shape[1]), F32)
    nt = (((1,), (1,)), ((), ()))
    outs = []
    for h in range(N_HEADS):
        hrows = slice(h * HEAD_DIM, (h + 1) * HEAD_DIM)
        kt = jnp.concatenate([kc_ref[0, h], knp[hrows, :]], axis=1).astype(BF16)
        sc = jnp.dot(q[:, hrows], kt, preferred_element_type=F32)
        sc = jnp.where(valid, sc, NEG_INF)
        m = jnp.maximum(jnp.max(sc, axis=-1, keepdims=True), -1e30)
        p = (cnt * jnp.exp(sc - m)).astype(BF16)
        vt = jnp.concatenate([vc_ref[0, h], vnp[hrows, :]], axis=1)
        vt = jnp.concatenate([vt, ones], axis=0).astype(BF16)
        o = jax.lax.dot_general(vt, p, nt, preferred_element_type=F32)
        outs.append((o[0:HEAD_DIM] * (1.0 / o[HEAD_DIM:HEAD_DIM + 1]))[:, 0:t])
    o_ref[0] = jnp.concatenate(outs, axis=0).T


def _attn_sample(q, k_new, v_new, k_cache_t, v_cache_t):
    b, t, d = q.shape
    w = k_cache_t.shape[3]
    assert t <= SUBLANES and w % LANES == 0
    cnt = np.zeros((2 * SUBLANES, w + LANES), np.float32)
    cnt[:t] = _branch_counts(t, w)
    tok = pl.BlockSpec((1, t, d), lambda bi: (bi, 0, 0))
    cache = pl.BlockSpec((1,) + k_cache_t.shape[1:], lambda bi: (bi, 0, 0, 0))
    return pl.pallas_call(
        functools.partial(_attn_sample_kernel, t=t),
        grid=(b,),
        in_specs=[tok, tok, tok, cache, cache, _resident(cnt.shape)],
        out_specs=tok,
        out_shape=jax.ShapeDtypeStruct((b, t, d), F32),
        scratch_shapes=[pltpu.VMEM((d, LANES), F32), pltpu.VMEM((d, LANES), F32)],
        compiler_params=pltpu.CompilerParams(dimension_semantics=("arbitrary",),
                                             vmem_limit_bytes=VMEM_LIMIT),
        name="attn_sample",
    )(q, k_new, v_new, k_cache_t, v_cache_t, jnp.asarray(cnt))


def _rglru_coeffs(xc, wa_ref, ba_ref, wx_ref, bx_ref, lam_ref):
    xb = xc.astype(BF16)
    rg = jax.nn.sigmoid(jnp.dot(xb, wa_ref[...], preferred_element_type=F32) + ba_ref[...])
    ig = jax.nn.sigmoid(jnp.dot(xb, wx_ref[...], preferred_element_type=F32) + bx_ref[...])
    log_a = (-LRU_C * jax.nn.softplus(-lam_ref[...])) * rg
    a = jnp.exp(log_a)
    return a, jnp.sqrt(-jnp.tanh(log_a) * (a * a + 1.0)) * ig * xc


def _rglru_tile(u, g, tail, h_prev, cw_ref, cb_ref, wa_ref, ba_ref, wx_ref, bx_ref, lam_ref):
    tc = u.shape[0]
    pad = SUBLANES
    full = jnp.concatenate([tail, u], axis=0)
    first = pad - (CONV_WIDTH - 1)
    xc = cb_ref[...] + cw_ref[0:1, :] * full[first:first + tc]
    for j in range(1, CONV_WIDTH):
        xc = xc + cw_ref[j:j + 1, :] * full[first + j:first + j + tc]
    new_tail = full[tc:tc + pad]
    a, b = _rglru_coeffs(xc, wa_ref, ba_ref, wx_ref, bx_ref, lam_ref)

    row = jax.lax.broadcasted_iota(jnp.int32, (SUBLANES, u.shape[1]), 0)
    h = h_prev
    hs = []
    for gi in range(tc // SUBLANES):
        ag = a[gi * SUBLANES:(gi + 1) * SUBLANES]
        bg = b[gi * SUBLANES:(gi + 1) * SUBLANES]
        for sh in (1, 2, 4):
            a_prev = pltpu.roll(ag, sh, axis=0)
            b_prev = pltpu.roll(bg, sh, axis=0)
            take = row >= sh
            bg = jnp.where(take, ag * b_prev + bg, bg)
            ag = jnp.where(take, ag * a_prev, ag)
        hg = ag * h + bg
        hs.append(hg)
        h = hg[SUBLANES - 1:SUBLANES, :]
    r = (jnp.concatenate(hs, axis=0) * jax.nn.gelu(g)).astype(BF16)
    return r, new_tail, h


def _rglru_steps_kernel(u_ref, g_ref, cbuf_ref, h0_ref, cw_ref, cb_ref, wa_ref, ba_ref, wx_ref, bx_ref,
                        lam_ref, r_ref, hlast_ref, tail_ref, *, t):
    nb = h0_ref.shape[0]
    taps = CONV_WIDTH - 1
    step_rows = lambda ti: pl.ds(ti, nb, stride=t)
    full = [cbuf_ref[j] for j in range(taps)] + [u_ref[step_rows(ti), :] for ti in range(t)]
    xcs = []
    for ti in range(t):
        xc = cb_ref[...] + cw_ref[0:1, :] * full[ti]
        for j in range(1, CONV_WIDTH):
            xc = xc + cw_ref[j:j + 1, :] * full[ti + j]
        xcs.append(xc)
    a, b = _rglru_coeffs(jnp.concatenate(xcs, axis=0), wa_ref, ba_ref, wx_ref, bx_ref, lam_ref)
    h = h0_ref[...]
    for ti in range(t):
        rows = slice(ti * nb, (ti + 1) * nb)
        h = a[rows] * h + b[rows]
        r_ref[step_rows(ti), :] = h * jax.nn.gelu(g_ref[step_rows(ti), :])
    hlast_ref[...] = h
    for j in range(taps):
        tail_ref[j] = full[t + j]


def _rglru_steps(u, g, conv_buf, h0, conv_w, conv_b, wa_bd, b_a, wx_bd, b_x, lam, *, t):
    n, r = u.shape
    nb = n // t
    taps = CONV_WIDTH - 1
    assert t >= taps
    col = lambda rows: pl.BlockSpec((rows, LANES), lambda c: (0, c))
    state = pl.BlockSpec((taps, nb, LANES), lambda c: (0, 0, c))
    diag = pl.BlockSpec((LANES, LANES), lambda c: (c, c))
    out, h_last, tail = pl.pallas_call(
        functools.partial(_rglru_steps_kernel, t=t),
        grid=(r // LANES,),
        in_specs=[col(n), col(n), state, col(nb), col(CONV_WIDTH), col(1), diag, col(1), diag, col(1), col(1)],
        out_specs=[col(n), col(nb), state],
        out_shape=[jax.ShapeDtypeStruct((n, r), F32), jax.ShapeDtypeStruct((nb, r), F32),
                   jax.ShapeDtypeStruct((taps, nb, r), F32)],
        compiler_params=pltpu.CompilerParams(dimension_semantics=("arbitrary",),
                                             vmem_limit_bytes=VMEM_LIMIT),
        name="rglru_steps",
    )(u, g, jnp.swapaxes(conv_buf, 0, 1), h0, conv_w, conv_b.reshape(1, r), wa_bd, b_a.reshape(1, r),
      wx_bd, b_x.reshape(1, r), lam.reshape(1, r))
    return out, h_last, jnp.swapaxes(tail, 0, 1)


def _out_ffn_kernel(x1_ref, attn_ref, r_ref, wo_ref, ln2_ref, wg_ref, wu_ref, wd_ref, lnf_ref, y_ref,
                    *, final_norm):
    mix = jnp.concatenate([attn_ref[...].astype(BF16), r_ref[...].astype(BF16)], axis=-1)
    x2 = x1_ref[...] + jnp.dot(mix, wo_ref[...], preferred_element_type=F32)
    xn = _rms(x2, ln2_ref[...]).astype(BF16)
    x3 = x2 + 0.5 * _swiglu(xn, wg_ref, wu_ref, wd_ref)
    y_ref[...] = _rms(x3, lnf_ref[...]) if final_norm else x3


def _out_ffn(x1, attn, r, wo, ln2, wg, wu, wd, lnf, *, final_norm, tm):
    n, dm = x1.shape
    dff = wg.shape[1]
    row = lambda w: pl.BlockSpec((tm, w), lambda i: (i, 0))
    return pl.pallas_call(
        functools.partial(_out_ffn_kernel, final_norm=final_norm),
        grid=(n // tm,),
        in_specs=[row(dm), row(D_ATTN), row(r.shape[1]), _resident(wo.shape), _resident((1, dm)),
                  _resident((dm, dff)), _resident((dm, dff)), _resident((dff, dm)), _resident((1, dm))],
        out_specs=row(dm),
        out_shape=jax.ShapeDtypeStruct((n, dm), F32),
        compiler_params=pltpu.CompilerParams(dimension_semantics=("arbitrary",),
                                             vmem_limit_bytes=VMEM_LIMIT),
        name="out_ffn",
    )(x1, attn, r, wo, ln2, wg, wu, wd, lnf)


def _block_diag(w):
    nb, c, d = w.shape
    return jnp.einsum("ncd,nm->ncmd", w, jnp.eye(nb, dtype=w.dtype)).reshape(nb * c, nb * d)


def _layer(x, conv_buf, h0, k_past, v_past, lw, lnf, *, final_norm, tm, tm_out):
    (ln1, w1g, w1u, w1d, ln_m, w_in, conv_w, conv_b, wa_bd, b_a, wx_bd, b_x, lam,
     w_out, ln2, w2g, w2u, w2d) = lw
    b, t, dm = x.shape
    r = conv_w.shape[-1]
    prompt = k_past is None
    keep = min(MAX_WINDOW, t) if prompt else t
    shp = lambda a: a.reshape(b, t, a.shape[-1])
    rnn_w = (conv_w, conv_b, wa_bd, b_a, wx_bd, b_x, lam)
    if prompt:
        x1, q, k, v, kt, vt, rnn, h_last, new_buf = _ffn_in_rglru(
            x.reshape(b * t, dm), ln1, w1g, w1u, w1d, ln_m, w_in, conv_buf, h0, *rnn_w, seq=t, keep=keep, tm=tm)
        attn = _attn_prompt(shp(q), shp(k), shp(v))
        k_state = jnp.transpose(kt.reshape(b, N_HEADS, HEAD_DIM, keep), (0, 3, 1, 2))
        v_state = jnp.transpose(vt.reshape(b, N_HEADS, HEAD_DIM, keep), (0, 3, 1, 2))
    else:
        x1, q, k, v, u, g = _ffn_in(x.reshape(b * t, dm), ln1, w1g, w1u, w1d, ln_m, w_in, tm=tm)
        attn = _attn_sample(shp(q), shp(k), shp(v),
                            jnp.transpose(k_past, (0, 2, 3, 1)), jnp.transpose(v_past, (0, 2, 3, 1)))
        rnn, h_last, new_buf = _rglru_steps(u, g, conv_buf, h0, *rnn_w, t=t)
        k_state = shp(k).reshape(b, t, N_HEADS, HEAD_DIM)
        v_state = shp(v).reshape(b, t, N_HEADS, HEAD_DIM)
    y = _out_ffn(x1, attn.reshape(b * t, D_ATTN), rnn.reshape(b * t, r), w_out, ln2, w2g, w2u, w2d, lnf,
                 final_norm=final_norm, tm=tm_out)
    return y.reshape(b, t, dm), k_state, v_state, h_last, new_buf


def kernel(x_prompt, x_sample, cache_k_win, cache_v_win, state_lru_h, state_lru_conv, ln_ffn1, w_ffn1_gate, w_ffn1_up, w_ffn1_down, ln_mix, w_in, conv_w, conv_b, w_gate_a, b_gate_a, w_gate_x, b_gate_x, lru_lambda, w_out, ln_ffn2, w_ffn2_gate, w_ffn2_up, w_ffn2_down, ln_final):
    depth = ln_ffn1.shape[0]
    dm = x_prompt.shape[-1]
    d_rnn = conv_w.shape[-1]
    bp = x_prompt.shape[0]
    xp, xs = x_prompt, x_sample
    outs = [[] for _ in range(8)]
    lnf = ln_final.reshape(1, dm)
    cfg = dict(tm=256, tm_out=512)
    for l in range(depth):
        lw = (ln_ffn1[l].reshape(1, dm), w_ffn1_gate[l].astype(BF16), w_ffn1_up[l].astype(BF16),
              w_ffn1_down[l].astype(BF16), ln_mix[l].reshape(1, dm), w_in[l].astype(BF16),
              conv_w[l], conv_b[l], _block_diag(w_gate_a[l]).astype(BF16), b_gate_a[l].reshape(-1),
              _block_diag(w_gate_x[l]).astype(BF16), b_gate_x[l].reshape(-1), lru_lambda[l],
              w_out[l].astype(BF16), ln_ffn2[l].reshape(1, dm), w_ffn2_gate[l].astype(BF16),
              w_ffn2_up[l].astype(BF16), w_ffn2_down[l].astype(BF16))
        last = l == depth - 1
        zero_buf = jnp.zeros((bp, CONV_WIDTH - 1, d_rnn), xp.dtype)
        zero_h = jnp.zeros((bp, d_rnn), xp.dtype)
        xp, kp, vp, hp, cp = _layer(xp, zero_buf, zero_h, None, None, lw, lnf, final_norm=last, **cfg)
        xs, kn, vn, hn, cn = _layer(xs, state_lru_conv[l], state_lru_h[l], cache_k_win[l],
                                    cache_v_win[l], lw, lnf, final_norm=last, **cfg)
        for lst, val in zip(outs, (kp, vp, hp, cp, kn, vn, hn, cn)):
            lst.append(val)
    return (xp, xs) + tuple(jnp.stack(lst) for lst in outs)
```

```python
import functools

import jax
import jax.numpy as jnp
import numpy as np
from jax.experimental import pallas as pl
from jax.experimental.pallas import tpu as pltpu

N_HEADS = 8
HEAD_DIM = 64
D_ATTN = N_HEADS * HEAD_DIM
CONV_WIDTH = 4
LRU_C = 8.0
DILATED = ((128, 1), (512, 4), (2048, 16))
MAX_WINDOW = 2048
BLK = 128
EPS = 1e-6

LANES = 128
SUBLANES = 8
VMEM_LIMIT = 56 * 1024 * 1024
VMEM_LIMIT_ATTN = 60 * 1024 * 1024

F32 = jnp.float32
BF16 = jnp.bfloat16
NEG_INF = float("-inf")


def _rms(x, g):
    return x * jax.lax.rsqrt(jnp.mean(x * x, axis=-1, keepdims=True) + EPS) * g


def _swiglu(xn, wg_ref, wu_ref, wd_ref):
    gate = jnp.dot(xn, wg_ref[...], preferred_element_type=F32)
    up = jnp.dot(xn, wu_ref[...], preferred_element_type=F32)
    h = (gate * jax.nn.sigmoid(gate) * up).astype(BF16)
    return jnp.dot(h, wd_ref[...], preferred_element_type=F32)


def _resident(shape):
    return pl.BlockSpec(shape, lambda *_: (0,) * len(shape), pipeline_mode=pl.Buffered(1))


def _ffn_in_body(x_ref, ln1_ref, wg_ref, wu_ref, wd_ref, lnm_ref, win_ref, x1_ref, q_ref, k_ref, v_ref):
    x = x_ref[...]
    xn = _rms(x, ln1_ref[...]).astype(BF16)
    x1 = x + 0.5 * _swiglu(xn, wg_ref, wu_ref, wd_ref)
    x1_ref[...] = x1
    zn = _rms(x1, lnm_ref[...]).astype(BF16)
    z = jnp.dot(zn, win_ref[...], preferred_element_type=F32)
    d = D_ATTN
    q_ref[...] = z[:, :d] * (HEAD_DIM ** -0.5)
    k_ref[...] = z[:, d:2 * d]
    v_ref[...] = z[:, 2 * d:3 * d]
    r = (z.shape[1] - 3 * d) // 2
    return z[:, 3 * d:3 * d + r], z[:, 3 * d + r:]


def _ffn_in_kernel(x_ref, ln1_ref, wg_ref, wu_ref, wd_ref, lnm_ref, win_ref,
                   x1_ref, q_ref, k_ref, v_ref, u_ref, g_ref):
    u, g = _ffn_in_body(x_ref, ln1_ref, wg_ref, wu_ref, wd_ref, lnm_ref, win_ref, x1_ref, q_ref, k_ref, v_ref)
    u_ref[...] = u
    g_ref[...] = g


def _ffn_in_rglru_kernel(x_ref, ln1_ref, wg_ref, wu_ref, wd_ref, lnm_ref, win_ref,
                         cbuf_ref, h0_ref, cw_ref, cb_ref, wa_ref, ba_ref, wx_ref, bx_ref, lam_ref,
                         x1_ref, q_ref, k_ref, v_ref, kt_ref, vt_ref, r_ref, hlast_ref, tail_ref,
                         u_s, g_s, tail_s, hcar, *, tiles_per_seq):
    i = pl.program_id(0)

    @pl.when(i == 0)
    def _():
        for ref in (u_s, g_s, tail_s, hcar):
            ref[...] = jnp.zeros_like(ref)

    starts = (i - 1) % tiles_per_seq == 0
    tail = jnp.where(starts, cbuf_ref[0], tail_s[...])
    h_prev = jnp.where(starts, h0_ref[0], hcar[...])
    r, new_tail, h = _rglru_tile(u_s[...], g_s[...], tail, h_prev, cw_ref, cb_ref, wa_ref, ba_ref,
                                 wx_ref, bx_ref, lam_ref)
    u, g = _ffn_in_body(x_ref, ln1_ref, wg_ref, wu_ref, wd_ref, lnm_ref, win_ref, x1_ref, q_ref, k_ref, v_ref)
    kt_ref[0] = k_ref[...].T
    vt_ref[0] = v_ref[...].T
    tail_s[...] = new_tail
    hcar[...] = h
    r_ref[...] = r
    hlast_ref[0] = h
    tail_ref[0] = new_tail
    u_s[...] = u
    g_s[...] = g


def _ffn_in(x, ln1, wg, wu, wd, lnm, win, *, tm):
    n, dm = x.shape
    dff = wg.shape[1]
    d_rnn = (win.shape[1] - 3 * D_ATTN) // 2
    row = lambda w: pl.BlockSpec((tm, w), lambda i: (i, 0))
    widths = (dm, D_ATTN, D_ATTN, D_ATTN, d_rnn, d_rnn)
    return pl.pallas_call(
        _ffn_in_kernel,
        grid=(n // tm,),
        in_specs=[row(dm), _resident((1, dm)), _resident((dm, dff)), _resident((dm, dff)),
                  _resident((dff, dm)), _resident((1, dm)), _resident(win.shape)],
        out_specs=[row(w) for w in widths],
        out_shape=[jax.ShapeDtypeStruct((n, w), F32) for w in widths],
        compiler_params=pltpu.CompilerParams(dimension_semantics=("arbitrary",),
                                             vmem_limit_bytes=VMEM_LIMIT),
        name="ffn_in",
    )(x, ln1, wg, wu, wd, lnm, win)


def _ffn_in_rglru(x, ln1, wg, wu, wd, lnm, win, conv_buf, h0, conv_w, conv_b, wa_bd, b_a, wx_bd, b_x, lam,
                  *, seq, keep, tm):
    n, dm = x.shape
    dff = wg.shape[1]
    r = conv_w.shape[-1]
    nb = n // seq
    tiles = n // tm
    tps = seq // tm
    first_kept = tps - keep // tm
    pad = SUBLANES
    cbuf = jnp.pad(conv_buf, ((0, 0), (pad - (CONV_WIDTH - 1), 0), (0, 0)))
    cur_tile = lambda i: jnp.minimum(i, tiles - 1)
    cur = lambda w: pl.BlockSpec((tm, w), lambda i: (cur_tile(i), 0))
    kept = pl.BlockSpec((1, D_ATTN, tm), lambda i: (cur_tile(i) // tps, 0,
                                                    jnp.maximum(cur_tile(i) % tps - first_kept, 0)))
    prev = lambda w: pl.BlockSpec((tm, w), lambda i: (jnp.maximum(i - 1, 0), 0))
    prev_seq = lambda rows: pl.BlockSpec((1, rows, r), lambda i: (jnp.maximum(i - 1, 0) // tps, 0, 0))
    vec = _resident((1, r))
    x1, q, k, v, kt, vt, rnn, h_last, tail = pl.pallas_call(
        functools.partial(_ffn_in_rglru_kernel, tiles_per_seq=tps),
        grid=(tiles + 1,),
        in_specs=[cur(dm), _resident((1, dm)), _resident((dm, dff)), _resident((dm, dff)),
                  _resident((dff, dm)), _resident((1, dm)), _resident(win.shape),
                  prev_seq(pad), prev_seq(1), _resident((CONV_WIDTH, r)), vec,
                  _resident((r, r)), vec, _resident((r, r)), vec, vec],
        out_specs=[cur(dm), cur(D_ATTN), cur(D_ATTN), cur(D_ATTN), kept, kept,
                   prev(r), prev_seq(1), prev_seq(pad)],
        out_shape=[jax.ShapeDtypeStruct((n, dm), F32)] + [jax.ShapeDtypeStruct((n, D_ATTN), F32)] * 3
                  + [jax.ShapeDtypeStruct((nb, D_ATTN, keep), F32)] * 2
                  + [jax.ShapeDtypeStruct((n, r), BF16), jax.ShapeDtypeStruct((nb, 1, r), F32),
                     jax.ShapeDtypeStruct((nb, pad, r), F32)],
        scratch_shapes=[pltpu.VMEM((tm, r), F32), pltpu.VMEM((tm, r), F32),
                        pltpu.VMEM((pad, r), F32), pltpu.VMEM((1, r), F32)],
        compiler_params=pltpu.CompilerParams(dimension_semantics=("arbitrary",),
                                             vmem_limit_bytes=VMEM_LIMIT),
        name="ffn_in_rglru",
    )(x, ln1, wg, wu, wd, lnm, win, cbuf, h0.reshape(nb, 1, r), conv_w, conv_b.reshape(1, r),
      wa_bd, b_a.reshape(1, r), wx_bd, b_x.reshape(1, r), lam.reshape(1, r))
    return x1, q, k, v, kt, vt, rnn, h_last.reshape(nb, r), tail[:, pad - (CONV_WIDTH - 1):]


FOLD = 16
BLOCK_UNROLL = 32


def _fold_rows(s):
    return s // FOLD + SUBLANES


def _branch_bias(dil):
    nch = FOLD // dil
    qs = BLK // nch
    ip = np.arange(BLK)
    kp = np.arange(2 * BLK)
    i = nch * (ip % qs) + ip // qs
    kk = nch * (kp % (2 * qs)) + kp // (2 * qs)
    delta = kk[None, :] - i[:, None]
    first = delta <= 0
    other = (delta >= 0) & (delta <= BLK)
    return np.where(np.concatenate([first, other], axis=0), 0.0, NEG_INF).astype(np.float32)


def _attn_prompt_kernel(q_ref, k_ref, v_ref, bias_ref, o_ref, qf, kf, vf, o_acc, m_acc, l_acc):
    s = q_ref.shape[1]
    pr = o_acc.shape[0] // FOLD
    prb = qf.shape[0] // FOLD
    tile = 2 * SUBLANES
    nt = (((1,), (1,)), ((), ()))
    low = jax.lax.broadcasted_iota(jnp.int32, (BLK, LANES), 1) < HEAD_DIM
    log2e = float(np.log2(np.e))

    def fold(mi, c):
        for grp in range(FOLD // SUBLANES):
            src = pl.ds(pl.multiple_of(mi * FOLD + grp * SUBLANES, SUBLANES), SUBLANES)
            dst = pl.ds(mi + grp * SUBLANES * pr, SUBLANES, stride=pr)
            o_acc[dst, :] = q_ref[0, src, :]
            m_acc[dst, :] = k_ref[0, src, :]
            l_acc[dst, :] = v_ref[0, src, :]
        return c

    jax.lax.fori_loop(0, s // FOLD, fold, 0, unroll=8)

    def pack(r, c):
        for ch in range(s // FOLD // BLK):
            src = pl.ds(pl.multiple_of(r * pr + ch * BLK, SUBLANES), BLK)
            dst = pl.ds(pl.multiple_of(r * prb + ch * BLK, tile), BLK)
            qf[dst, :] = (o_acc[src, :] * log2e).astype(BF16)
            kf[dst, :] = m_acc[src, :].astype(BF16)
            vf[dst, :] = l_acc[src, :].astype(BF16)
        return c

    jax.lax.fori_loop(0, FOLD, pack, 0)

    n_branches = len(DILATED)
    for bi, (_, dil) in enumerate(DILATED):
        natural = dil == 1
        nch = 1 if natural else FOLD // dil
        qs = BLK // nch
        nblk = s // (dil * BLK)
        first, last = bi == 0, bi == n_branches - 1
        assert first or not natural

        def block(idx, c, bi=bi, dil=dil, natural=natural, nch=nch, qs=qs, nblk=nblk, first=first, last=last):
            rd = idx // nblk
            j = idx % nblk
            prev = jnp.maximum(j - 1, 0)

            def gather(ref, base, n):
                if natural:
                    return ref[0, pl.ds(pl.multiple_of(base, BLK), n), :]
                return jnp.concatenate(
                    [ref[pl.ds(pl.multiple_of(rd * prb + base + a * dil * prb, tile), n), :] for a in range(nch)],
                    axis=0)

            if natural:
                q2 = (gather(q_ref, qs * j, qs) * log2e).astype(BF16)
                k2 = gather(k_ref, qs * prev, 2 * qs).astype(BF16)
                v2 = gather(v_ref, qs * prev, 2 * qs).astype(BF16)
            else:
                q2 = gather(qf, qs * j, qs)
                k2 = gather(kf, qs * prev, 2 * qs)
                v2 = gather(vf, qs * prev, 2 * qs)
            brow = (2 * bi + jnp.minimum(j, 1)) * BLK
            bias = bias_ref[pl.ds(pl.multiple_of(brow, BLK), BLK), :]
            res = []
            for sel in (low, ~low):
                qm = jnp.where(sel, q2, jnp.zeros_like(q2))
                sc = jax.lax.dot_general(qm, k2, nt, preferred_element_type=F32) + bias
                m = jnp.max(sc, axis=-1, keepdims=True)
                p = jnp.exp2(sc - m)
                l = jnp.sum(p, axis=-1, keepdims=True)
                o = jnp.dot(p.astype(BF16), v2, preferred_element_type=F32)
                res.append((o, m, l))
            o_b, m_b, l_b = (jnp.where(low, x0, x1) for x0, x1 in zip(*res))
            if natural:
                for g in range(BLK // SUBLANES):
                    rows = pl.ds((g % 2) * SUBLANES * pr + (BLK // FOLD) * j + g // 2, SUBLANES, stride=pr)
                    part = slice(g * SUBLANES, (g + 1) * SUBLANES)
                    o_acc[rows, :] = o_b[part]
                    m_acc[rows, :] = m_b[part]
                    l_acc[rows, :] = l_b[part]
                return c
            for a in range(nch):
                rows = pl.ds(pl.multiple_of(rd * pr + qs * j + a * dil * pr, SUBLANES), qs)
                part = slice(a * qs, (a + 1) * qs)
                m_old = m_acc[rows, :]
                m_new = jnp.maximum(m_old, m_b[part])
                keep = m_old >= m_b[part]
                e_min = jnp.exp2(jnp.minimum(m_old, m_b[part]) - m_new)
                e_old = jnp.where(keep, 1.0, e_min)
                e_b = jnp.where(keep, e_min, 1.0)
                o_new = o_acc[rows, :] * e_old + o_b[part] * e_b
                l_new = l_acc[rows, :] * e_old + l_b[part] * e_b
                if last:
                    o_acc[rows, :] = o_new * (1.0 / l_new)
                else:
                    o_acc[rows, :] = o_new
                    m_acc[rows, :] = m_new
                    l_acc[rows, :] = l_new
            return c

        jax.lax.fori_loop(0, dil * nblk, block, 0, unroll=BLOCK_UNROLL)

    def unfold(mi, c):
        rows = o_acc[pl.ds(mi, FOLD, stride=pr), :]
        o_ref[0, pl.ds(pl.multiple_of(mi * FOLD, FOLD), FOLD), :] = rows.astype(BF16)
        return c

    jax.lax.fori_loop(0, s // FOLD, unfold, 0, unroll=4)


def _attn_prompt(q, k, v):
    b, s, d = q.shape
    assert s % (FOLD * BLK) == 0 and s // FOLD >= 2 * BLK and all(FOLD % dil == 0 for _, dil in DILATED)
    assert DILATED[0][1] == 1 and FOLD == 2 * SUBLANES
    bias = np.concatenate([_branch_bias(FOLD if dil == 1 else dil) for _, dil in DILATED], axis=0)
    seq = pl.BlockSpec((1, s, LANES), lambda bi, hp: (bi, 0, hp))
    folded = pltpu.VMEM((FOLD * (s // FOLD + 2 * SUBLANES), LANES), BF16)
    acc = pltpu.VMEM((FOLD * _fold_rows(s), LANES), F32)
    return pl.pallas_call(
        _attn_prompt_kernel,
        grid=(b, d // LANES),
        in_specs=[seq, seq, seq, _resident(bias.shape)],
        out_specs=pl.BlockSpec((1, s, LANES), lambda bi, hp: (bi, 0, hp)),
        out_shape=jax.ShapeDtypeStruct((b, s, d), BF16),
        scratch_shapes=[folded] * 3 + [acc] * 3,
        compiler_params=pltpu.CompilerParams(dimension_semantics=("arbitrary", "arbitrary"),
                                             vmem_limit_bytes=VMEM_LIMIT_ATTN),
        name="attn_prompt",
    )(q, k, v, jnp.asarray(bias))


def _branch_counts(t, w_buf):
    i = np.arange(t)[:, None]
    c = np.arange(w_buf)[None, :]
    n = np.arange(LANES)[None, :]
    cnt_c = np.zeros((t, w_buf), np.float32)
    cnt_n = np.zeros((t, LANES), np.float32)
    for win, dil in DILATED:
        dist = w_buf + i - c
        cnt_c += ((dist % dil == 0) & (dist >= dil) & (dist <= win)).astype(np.float32)
        dist = i - n
        cnt_n += ((n < t) & (dist >= 0) & (dist % dil == 0) & (dist <= win)).astype(np.float32)
    return np.concatenate([cnt_c, cnt_n], axis=1)


def _attn_sample_kernel(q_ref, kn_ref, vn_ref, kc_ref, vc_ref, cnt_ref, o_ref, knp, vnp, *, t):
    rows = 2 * SUBLANES

    @pl.when(pl.program_id(0) == 0)
    def _():
        knp[...] = jnp.zeros_like(knp)
        vnp[...] = jnp.zeros_like(vnp)

    knp[:, 0:t] = kn_ref[0].T
    vnp[:, 0:t] = vn_ref[0].T
    cnt = cnt_ref[...]
    valid = cnt > 0.0
    q = jnp.concatenate([q_ref[0], jnp.zeros((rows - t, D_ATTN), F32)], axis=0).astype(BF16)
    ones = jnp.ones((SUBLANES, cnt.shape[1]), F32)
    nt = (((1,), (1,)), ((), ()))
    outs = []
    for h in range(N_HEADS):
        hrows = slice(h * HEAD_DIM, (h + 1) * HEAD_DIM)
        kt = jnp.concatenate([kc_ref[0, h], knp[hrows, :]], axis=1).astype(BF16)
        sc = jnp.dot(q[:, hrows], kt, preferred_element_type=F32)
        sc = jnp.where(valid, sc, NEG_INF)
        m = jnp.maximum(jnp.max(sc, axis=-1, keepdims=True), -1e30)
        p = (cnt * jnp.exp(sc - m)).astype(BF16)
        vt = jnp.concatenate([vc_ref[0, h], vnp[hrows, :]], axis=1)
        vt = jnp.concatenate([vt, ones], axis=0).astype(BF16)
        o = jax.lax.dot_general(vt, p, nt, preferred_element_type=F32)
        outs.append((o[0:HEAD_DIM] * (1.0 / o[HEAD_DIM:HEAD_DIM + 1]))[:, 0:t])
    o_ref[0] = jnp.concatenate(outs, axis=0).T


def _attn_sample(q, k_new, v_new, k_cache_t, v_cache_t):
    b, t, d = q.shape
    w = k_cache_t.shape[3]
    assert t <= SUBLANES and w % LANES == 0
    cnt = np.zeros((2 * SUBLANES, w + LANES), np.float32)
    cnt[:t] = _branch_counts(t, w)
    tok = pl.BlockSpec((1, t, d), lambda bi: (bi, 0, 0))
    cache = pl.BlockSpec((1,) + k_cache_t.shape[1:], lambda bi: (bi, 0, 0, 0))
    return pl.pallas_call(
        functools.partial(_attn_sample_kernel, t=t),
        grid=(b,),
        in_specs=[tok, tok, tok, cache, cache, _resident(cnt.shape)],
        out_specs=tok,
        out_shape=jax.ShapeDtypeStruct((b, t, d), F32),
        scratch_shapes=[pltpu.VMEM((d, LANES), F32), pltpu.VMEM((d, LANES), F32)],
        compiler_params=pltpu.CompilerParams(dimension_semantics=("arbitrary",),
                                             vmem_limit_bytes=VMEM_LIMIT),
        name="attn_sample",
    )(q, k_new, v_new, k_cache_t, v_cache_t, jnp.asarray(cnt))


def _rglru_coeffs(xc, wa_ref, ba_ref, wx_ref, bx_ref, lam_ref):
    xb = xc.astype(BF16)
    rg = jax.nn.sigmoid(jnp.dot(xb, wa_ref[...], preferred_element_type=F32) + ba_ref[...])
    ig = jax.nn.sigmoid(jnp.dot(xb, wx_ref[...], preferred_element_type=F32) + bx_ref[...])
    log_a = (-LRU_C * jax.nn.softplus(-lam_ref[...])) * rg
    a = jnp.exp(log_a)
    return a, jnp.sqrt(-jnp.tanh(log_a) * (a * a + 1.0)) * ig * xc


def _rglru_tile(u, g, tail, h_prev, cw_ref, cb_ref, wa_ref, ba_ref, wx_ref, bx_ref, lam_ref):
    tc = u.shape[0]
    pad = SUBLANES
    full = jnp.concatenate([tail, u], axis=0)
    first = pad - (CONV_WIDTH - 1)

    def tap(j):
        return pltpu.roll(full, (pad + tc - first - j) % (pad + tc), axis=0)[0:tc]

    xc = cb_ref[...] + cw_ref[0:1, :] * tap(0)
    for j in range(1, CONV_WIDTH):
        xc = xc + cw_ref[j:j + 1, :] * tap(j)
    new_tail = full[tc:tc + pad]
    a, b = _rglru_coeffs(xc, wa_ref, ba_ref, wx_ref, bx_ref, lam_ref)

    row = jax.lax.broadcasted_iota(jnp.int32, (SUBLANES, u.shape[1]), 0)
    h = h_prev
    hs = []
    for gi in range(tc // SUBLANES):
        ag = a[gi * SUBLANES:(gi + 1) * SUBLANES]
        bg = b[gi * SUBLANES:(gi + 1) * SUBLANES]
        for sh in (1, 2, 4):
            a_prev = pltpu.roll(ag, sh, axis=0)
            b_prev = pltpu.roll(bg, sh, axis=0)
            take = row >= sh
            bg = jnp.where(take, ag * b_prev + bg, bg)
            ag = jnp.where(take, ag * a_prev, ag)
        hg = ag * h + bg
        hs.append(hg)
        h = hg[SUBLANES - 1:SUBLANES, :]
    r = (jnp.concatenate(hs, axis=0) * jax.nn.gelu(g)).astype(BF16)
    return r, new_tail, h


def _rglru_steps_kernel(u_ref, g_ref, cbuf_ref, h0_ref, cw_ref, cb_ref, wa_ref, ba_ref, wx_ref, bx_ref,
                        lam_ref, r_ref, hlast_ref, tail_ref, *, t):
    nb = h0_ref.shape[0]
    taps = CONV_WIDTH - 1
    step_rows = lambda ti: pl.ds(ti, nb, stride=t)
    full = [cbuf_ref[j] for j in range(taps)] + [u_ref[step_rows(ti), :] for ti in range(t)]
    xcs = []
    for ti in range(t):
        xc = cb_ref[...] + cw_ref[0:1, :] * full[ti]
        for j in range(1, CONV_WIDTH):
            xc = xc + cw_ref[j:j + 1, :] * full[ti + j]
        xcs.append(xc)
    a, b = _rglru_coeffs(jnp.concatenate(xcs, axis=0), wa_ref, ba_ref, wx_ref, bx_ref, lam_ref)
    h = h0_ref[...]
    for ti in range(t):
        rows = slice(ti * nb, (ti + 1) * nb)
        h = a[rows] * h + b[rows]
        r_ref[step_rows(ti), :] = h * jax.nn.gelu(g_ref[step_rows(ti), :])
    hlast_ref[...] = h
    for j in range(taps):
        tail_ref[j] = full[t + j]


def _rglru_steps(u, g, conv_buf, h0, conv_w, conv_b, wa_bd, b_a, wx_bd, b_x, lam, *, t):
    n, r = u.shape
    nb = n // t
    taps = CONV_WIDTH - 1
    assert t >= taps
    col = lambda rows: pl.BlockSpec((rows, LANES), lambda c: (0, c))
    state = pl.BlockSpec((taps, nb, LANES), lambda c: (0, 0, c))
    diag = pl.BlockSpec((LANES, LANES), lambda c: (c, c))
    out, h_last, tail = pl.pallas_call(
        functools.partial(_rglru_steps_kernel, t=t),
        grid=(r // LANES,),
        in_specs=[col(n), col(n), state, col(nb), col(CONV_WIDTH), col(1), diag, col(1), diag, col(1), col(1)],
        out_specs=[col(n), col(nb), state],
        out_shape=[jax.ShapeDtypeStruct((n, r), F32), jax.ShapeDtypeStruct((nb, r), F32),
                   jax.ShapeDtypeStruct((taps, nb, r), F32)],
        compiler_params=pltpu.CompilerParams(dimension_semantics=("arbitrary",),
                                             vmem_limit_bytes=VMEM_LIMIT),
        name="rglru_steps",
    )(u, g, jnp.swapaxes(conv_buf, 0, 1), h0, conv_w, conv_b.reshape(1, r), wa_bd, b_a.reshape(1, r),
      wx_bd, b_x.reshape(1, r), lam.reshape(1, r))
    return out, h_last, jnp.swapaxes(tail, 0, 1)


def _out_ffn_kernel(x1_ref, attn_ref, r_ref, wo_ref, ln2_ref, wg_ref, wu_ref, wd_ref, lnf_ref, y_ref,
                    *, final_norm):
    mix = jnp.concatenate([attn_ref[...].astype(BF16), r_ref[...].astype(BF16)], axis=-1)
    x2 = x1_ref[...] + jnp.dot(mix, wo_ref[...], preferred_element_type=F32)
    xn = _rms(x2, ln2_ref[...]).astype(BF16)
    x3 = x2 + 0.5 * _swiglu(xn, wg_ref, wu_ref, wd_ref)
    y_ref[...] = _rms(x3, lnf_ref[...]) if final_norm else x3


def _out_ffn(x1, attn, r, wo, ln2, wg, wu, wd, lnf, *, final_norm, tm):
    n, dm = x1.shape
    dff = wg.shape[1]
    row = lambda w: pl.BlockSpec((tm, w), lambda i: (i, 0))
    return pl.pallas_call(
        functools.partial(_out_ffn_kernel, final_norm=final_norm),
        grid=(n // tm,),
        in_specs=[row(dm), row(D_ATTN), row(r.shape[1]), _resident(wo.shape), _resident((1, dm)),
                  _resident((dm, dff)), _resident((dm, dff)), _resident((dff, dm)), _resident((1, dm))],
        out_specs=row(dm),
        out_shape=jax.ShapeDtypeStruct((n, dm), F32),
        compiler_params=pltpu.CompilerParams(dimension_semantics=("arbitrary",),
                                             vmem_limit_bytes=VMEM_LIMIT),
        name="out_ffn",
    )(x1, attn, r, wo, ln2, wg, wu, wd, lnf)


def _block_diag(w):
    nb, c, d = w.shape
    return jnp.einsum("ncd,nm->ncmd", w, jnp.eye(nb, dtype=w.dtype)).reshape(nb * c, nb * d)


def _layer(x, conv_buf, h0, k_past, v_past, lw, lnf, *, final_norm, tm, tm_out):
    (ln1, w1g, w1u, w1d, ln_m, w_in, conv_w, conv_b, wa_bd, b_a, wx_bd, b_x, lam,
     w_out, ln2, w2g, w2u, w2d) = lw
    b, t, dm = x.shape
    r = conv_w.shape[-1]
    prompt = k_past is None
    keep = min(MAX_WINDOW, t) if prompt else t
    shp = lambda a: a.reshape(b, t, a.shape[-1])
    rnn_w = (conv_w, conv_b, wa_bd, b_a, wx_bd, b_x, lam)
    if prompt:
        x1, q, k, v, kt, vt, rnn, h_last, new_buf = _ffn_in_rglru(
            x.reshape(b * t, dm), ln1, w1g, w1u, w1d, ln_m, w_in, conv_buf, h0, *rnn_w, seq=t, keep=keep, tm=tm)
        attn = _attn_prompt(shp(q), shp(k), shp(v))
        k_state = jnp.transpose(kt.reshape(b, N_HEADS, HEAD_DIM, keep), (0, 3, 1, 2))
        v_state = jnp.transpose(vt.reshape(b, N_HEADS, HEAD_DIM, keep), (0, 3, 1, 2))
    else:
        x1, q, k, v, u, g = _ffn_in(x.reshape(b * t, dm), ln1, w1g, w1u, w1d, ln_m, w_in, tm=tm)
        attn = _attn_sample(shp(q), shp(k), shp(v),
                            jnp.transpose(k_past, (0, 2, 3, 1)), jnp.transpose(v_past, (0, 2, 3, 1)))
        rnn, h_last, new_buf = _rglru_steps(u, g, conv_buf, h0, *rnn_w, t=t)
        k_state = shp(k).reshape(b, t, N_HEADS, HEAD_DIM)
        v_state = shp(v).reshape(b, t, N_HEADS, HEAD_DIM)
    y = _out_ffn(x1, attn.reshape(b * t, D_ATTN), rnn.reshape(b * t, r), w_out, ln2, w2g, w2u, w2d, lnf,
                 final_norm=final_norm, tm=tm_out)
    return y.reshape(b, t, dm), k_state, v_state, h_last, new_buf


def kernel(x_prompt, x_sample, cache_k_win, cache_v_win, state_lru_h, state_lru_conv, ln_ffn1, w_ffn1_gate, w_ffn1_up, w_ffn1_down, ln_mix, w_in, conv_w, conv_b, w_gate_a, b_gate_a, w_gate_x, b_gate_x, lru_lambda, w_out, ln_ffn2, w_ffn2_gate, w_ffn2_up, w_ffn2_down, ln_final):
    depth = ln_ffn1.shape[0]
    dm = x_prompt.shape[-1]
    d_rnn = conv_w.shape[-1]
    bp = x_prompt.shape[0]
    xp, xs = x_prompt, x_sample
    outs = [[] for _ in range(8)]
    lnf = ln_final.reshape(1, dm)
    cfg = dict(tm=256, tm_out=512)
    for l in range(depth):
        lw = (ln_ffn1[l].reshape(1, dm), w_ffn1_gate[l].astype(BF16), w_ffn1_up[l].astype(BF16),
              w_ffn1_down[l].astype(BF16), ln_mix[l].reshape(1, dm), w_in[l].astype(BF16),
              conv_w[l], conv_b[l], _block_diag(w_gate_a[l]).astype(BF16), b_gate_a[l].reshape(-1),
              _block_diag(w_gate_x[l]).astype(BF16), b_gate_x[l].reshape(-1), lru_lambda[l],
              w_out[l].astype(BF16), ln_ffn2[l].reshape(1, dm), w_ffn2_gate[l].astype(BF16),
              w_ffn2_up[l].astype(BF16), w_ffn2_down[l].astype(BF16))
        last = l == depth - 1
        zero_buf = jnp.zeros((bp, CONV_WIDTH - 1, d_rnn), xp.dtype)
        zero_h = jnp.zeros((bp, d_rnn), xp.dtype)
        xp, kp, vp, hp, cp = _layer(xp, zero_buf, zero_h, None, None, lw, lnf, final_norm=last, **cfg)
        xs, kn, vn, hn, cn = _layer(xs, state_lru_conv[l], state_lru_h[l], cache_k_win[l],
                                    cache_v_win[l], lw, lnf, final_norm=last, **cfg)
        for lst, val in zip(outs, (kp, vp, hp, cp, kn, vn, hn, cn)):
            lst.append(val)
    return (xp, xs) + tuple(jnp.stack(lst) for lst in outs)
```

```python
import functools

import jax
import jax.numpy as jnp
import numpy as np
from jax.experimental import pallas as pl
from jax.experimental.pallas import tpu as pltpu

N_HEADS = 8
HEAD_DIM = 64
D_ATTN = N_HEADS * HEAD_DIM
CONV_WIDTH = 4
LRU_C = 8.0
DILATED = ((128, 1), (512, 4), (2048, 16))
MAX_WINDOW = 2048
BLK = 128
EPS = 1e-6

LANES = 128
SUBLANES = 8
VMEM_LIMIT = 56 * 1024 * 1024
VMEM_LIMIT_ATTN = 60 * 1024 * 1024

F32 = jnp.float32
BF16 = jnp.bfloat16
NEG_INF = float("-inf")


def _rms(x, g):
    return x * jax.lax.rsqrt(jnp.mean(x * x, axis=-1, keepdims=True) + EPS) * g


def _swiglu(xn, wg_ref, wu_ref, wd_ref):
    gate = jnp.dot(xn, wg_ref[...], preferred_element_type=F32)
    up = jnp.dot(xn, wu_ref[...], preferred_element_type=F32)
    h = (gate * jax.nn.sigmoid(gate) * up).astype(BF16)
    return jnp.dot(h, wd_ref[...], preferred_element_type=F32)


def _resident(shape):
    return pl.BlockSpec(shape, lambda *_: (0,) * len(shape), pipeline_mode=pl.Buffered(1))


def _ffn_in_body(x_ref, ln1_ref, wg_ref, wu_ref, wd_ref, lnm_ref, win_ref, x1_ref, q_ref, k_ref, v_ref):
    x = x_ref[...]
    xn = _rms(x, ln1_ref[...]).astype(BF16)
    x1 = x + 0.5 * _swiglu(xn, wg_ref, wu_ref, wd_ref)
    x1_ref[...] = x1
    zn = _rms(x1, lnm_ref[...]).astype(BF16)
    z = jnp.dot(zn, win_ref[...], preferred_element_type=F32)
    d = D_ATTN
    for i, ref in enumerate((q_ref, k_ref, v_ref)):
        t = z[:, i * d:(i + 1) * d] * (HEAD_DIM ** -0.5) if i == 0 else z[:, i * d:(i + 1) * d]
        if len(ref.shape) == 2:
            ref[...] = t
        else:
            for hp in range(ref.shape[0]):
                ref[hp] = t[:, hp * LANES:(hp + 1) * LANES]
    r = (z.shape[1] - 3 * d) // 2
    return z[:, 3 * d:3 * d + r], z[:, 3 * d + r:]


def _ffn_in_kernel(x_ref, ln1_ref, wg_ref, wu_ref, wd_ref, lnm_ref, win_ref,
                   x1_ref, q_ref, k_ref, v_ref, u_ref, g_ref):
    u, g = _ffn_in_body(x_ref, ln1_ref, wg_ref, wu_ref, wd_ref, lnm_ref, win_ref, x1_ref, q_ref, k_ref, v_ref)
    u_ref[...] = u
    g_ref[...] = g


def _ffn_in_rglru_kernel(x_ref, ln1_ref, wg_ref, wu_ref, wd_ref, lnm_ref, win_ref,
                         cbuf_ref, h0_ref, cw_ref, cb_ref, wa_ref, ba_ref, wx_ref, bx_ref, lam_ref,
                         x1_ref, q_ref, k_ref, v_ref, kt_ref, vt_ref, r_ref, hlast_ref, tail_ref,
                         u_s, g_s, tail_s, hcar, *, tiles_per_seq):
    i = pl.program_id(0)

    @pl.when(i == 0)
    def _():
        for ref in (u_s, g_s, tail_s, hcar):
            ref[...] = jnp.zeros_like(ref)

    starts = (i - 1) % tiles_per_seq == 0
    tail = jnp.where(starts, cbuf_ref[0], tail_s[...])
    h_prev = jnp.where(starts, h0_ref[0], hcar[...])
    r, new_tail, h = _rglru_tile(u_s[...], g_s[...], tail, h_prev, cw_ref, cb_ref, wa_ref, ba_ref,
                                 wx_ref, bx_ref, lam_ref)
    u, g = _ffn_in_body(x_ref, ln1_ref, wg_ref, wu_ref, wd_ref, lnm_ref, win_ref, x1_ref, q_ref, k_ref, v_ref)
    for hp in range(k_ref.shape[0]):
        kt_ref[0, hp * LANES:(hp + 1) * LANES, :] = k_ref[hp].T
        vt_ref[0, hp * LANES:(hp + 1) * LANES, :] = v_ref[hp].T
    tail_s[...] = new_tail
    hcar[...] = h
    r_ref[...] = r
    hlast_ref[0] = h
    tail_ref[0] = new_tail
    u_s[...] = u
    g_s[...] = g


def _ffn_in(x, ln1, wg, wu, wd, lnm, win, *, tm):
    n, dm = x.shape
    dff = wg.shape[1]
    d_rnn = (win.shape[1] - 3 * D_ATTN) // 2
    row = lambda w: pl.BlockSpec((tm, w), lambda i: (i, 0))
    widths = (dm, D_ATTN, D_ATTN, D_ATTN, d_rnn, d_rnn)
    return pl.pallas_call(
        _ffn_in_kernel,
        grid=(n // tm,),
        in_specs=[row(dm), _resident((1, dm)), _resident((dm, dff)), _resident((dm, dff)),
                  _resident((dff, dm)), _resident((1, dm)), _resident(win.shape)],
        out_specs=[row(w) for w in widths],
        out_shape=[jax.ShapeDtypeStruct((n, w), F32) for w in widths],
        compiler_params=pltpu.CompilerParams(dimension_semantics=("arbitrary",),
                                             vmem_limit_bytes=VMEM_LIMIT),
        name="ffn_in",
    )(x, ln1, wg, wu, wd, lnm, win)


def _ffn_in_rglru(x, ln1, wg, wu, wd, lnm, win, conv_buf, h0, conv_w, conv_b, wa_bd, b_a, wx_bd, b_x, lam,
                  *, seq, keep, tm):
    n, dm = x.shape
    dff = wg.shape[1]
    r = conv_w.shape[-1]
    nb = n // seq
    tiles = n // tm
    tps = seq // tm
    first_kept = tps - keep // tm
    pad = SUBLANES
    cbuf = jnp.pad(conv_buf, ((0, 0), (pad - (CONV_WIDTH - 1), 0), (0, 0)))
    cur_tile = lambda i: jnp.minimum(i, tiles - 1)
    cur = lambda w: pl.BlockSpec((tm, w), lambda i: (cur_tile(i), 0))
    kept = pl.BlockSpec((1, D_ATTN, tm), lambda i: (cur_tile(i) // tps, 0,
                                                    jnp.maximum(cur_tile(i) % tps - first_kept, 0)))
    npairs = D_ATTN // LANES
    pairs = pl.BlockSpec((npairs, tm, LANES), lambda i: (0, cur_tile(i), 0))
    prev = lambda w: pl.BlockSpec((tm, w), lambda i: (jnp.maximum(i - 1, 0), 0))
    prev_seq = lambda rows: pl.BlockSpec((1, rows, r), lambda i: (jnp.maximum(i - 1, 0) // tps, 0, 0))
    vec = _resident((1, r))
    x1, q, k, v, kt, vt, rnn, h_last, tail = pl.pallas_call(
        functools.partial(_ffn_in_rglru_kernel, tiles_per_seq=tps),
        grid=(tiles + 1,),
        in_specs=[cur(dm), _resident((1, dm)), _resident((dm, dff)), _resident((dm, dff)),
                  _resident((dff, dm)), _resident((1, dm)), _resident(win.shape),
                  prev_seq(pad), prev_seq(1), _resident((CONV_WIDTH, r)), vec,
                  _resident((r, r)), vec, _resident((r, r)), vec, vec],
        out_specs=[cur(dm), pairs, pairs, pairs, kept, kept,
                   prev(r), prev_seq(1), prev_seq(pad)],
        out_shape=[jax.ShapeDtypeStruct((n, dm), F32)] + [jax.ShapeDtypeStruct((npairs, n, LANES), F32)] * 3
                  + [jax.ShapeDtypeStruct((nb, D_ATTN, keep), F32)] * 2
                  + [jax.ShapeDtypeStruct((n, r), BF16), jax.ShapeDtypeStruct((nb, 1, r), F32),
                     jax.ShapeDtypeStruct((nb, pad, r), F32)],
        scratch_shapes=[pltpu.VMEM((tm, r), F32), pltpu.VMEM((tm, r), F32),
                        pltpu.VMEM((pad, r), F32), pltpu.VMEM((1, r), F32)],
        compiler_params=pltpu.CompilerParams(dimension_semantics=("arbitrary",),
                                             vmem_limit_bytes=VMEM_LIMIT),
        name="ffn_in_rglru",
    )(x, ln1, wg, wu, wd, lnm, win, cbuf, h0.reshape(nb, 1, r), conv_w, conv_b.reshape(1, r),
      wa_bd, b_a.reshape(1, r), wx_bd, b_x.reshape(1, r), lam.reshape(1, r))
    return x1, q, k, v, kt, vt, rnn, h_last.reshape(nb, r), tail[:, pad - (CONV_WIDTH - 1):]


FOLD = 16
BLOCK_UNROLL = 32


def _fold_rows(s):
    return s // FOLD + SUBLANES


def _branch_bias(dil):
    nch = FOLD // dil
    qs = BLK // nch
    ip = np.arange(BLK)
    kp = np.arange(2 * BLK)
    i = nch * (ip % qs) + ip // qs
    kk = nch * (kp % (2 * qs)) + kp // (2 * qs)
    delta = kk[None, :] - i[:, None]
    first = delta <= 0
    other = (delta >= 0) & (delta <= BLK)
    return np.where(np.concatenate([first, other], axis=0), 0.0, NEG_INF).astype(np.float32)


def _attn_prompt_kernel(q_ref, k_ref, v_ref, bias_ref, o_ref, qf, kf, vf, o_acc, m_acc, l_acc):
    s = q_ref.shape[1]
    pr = o_acc.shape[0] // FOLD
    prb = qf.shape[0] // FOLD
    tile = 2 * SUBLANES
    nt = (((1,), (1,)), ((), ()))
    low = jax.lax.broadcasted_iota(jnp.int32, (BLK, LANES), 1) < HEAD_DIM
    log2e = float(np.log2(np.e))

    def fold(mi, c):
        for grp in range(FOLD // SUBLANES):
            src = pl.ds(pl.multiple_of(mi * FOLD + grp * SUBLANES, SUBLANES), SUBLANES)
            dst = pl.ds(mi + grp * SUBLANES * pr, SUBLANES, stride=pr)
            o_acc[dst, :] = q_ref[0, src, :]
            m_acc[dst, :] = k_ref[0, src, :]
            l_acc[dst, :] = v_ref[0, src, :]
        return c

    jax.lax.fori_loop(0, s // FOLD, fold, 0, unroll=8)

    def pack(r, c):
        for ch in range(s // FOLD // BLK):
            src = pl.ds(pl.multiple_of(r * pr + ch * BLK, SUBLANES), BLK)
            dst = pl.ds(pl.multiple_of(r * prb + ch * BLK, tile), BLK)
            qf[dst, :] = (o_acc[src, :] * log2e).astype(BF16)
            kf[dst, :] = m_acc[src, :].astype(BF16)
            vf[dst, :] = l_acc[src, :].astype(BF16)
        return c

    jax.lax.fori_loop(0, FOLD, pack, 0)

    n_branches = len(DILATED)
    for bi, (_, dil) in enumerate(DILATED):
        natural = dil == 1
        nch = 1 if natural else FOLD // dil
        qs = BLK // nch
        nblk = s // (dil * BLK)
        first, last = bi == 0, bi == n_branches - 1
        assert first or not natural

        def block(idx, c, bi=bi, dil=dil, natural=natural, nch=nch, qs=qs, nblk=nblk, first=first, last=last):
            rd = idx // nblk
            j = idx % nblk
            prev = jnp.maximum(j - 1, 0)

            def gather(ref, base, n):
                if natural:
                    return ref[0, pl.ds(pl.multiple_of(base, BLK), n), :]
                return jnp.concatenate(
                    [ref[pl.ds(pl.multiple_of(rd * prb + base + a * dil * prb, tile), n), :] for a in range(nch)],
                    axis=0)

            if natural:
                q2 = (gather(q_ref, qs * j, qs) * log2e).astype(BF16)
                k2 = gather(k_ref, qs * prev, 2 * qs).astype(BF16)
                v2 = gather(v_ref, qs * prev, 2 * qs).astype(BF16)
            else:
                q2 = gather(qf, qs * j, qs)
                k2 = gather(kf, qs * prev, 2 * qs)
                v2 = gather(vf, qs * prev, 2 * qs)
            brow = (2 * bi + jnp.minimum(j, 1)) * BLK
            bias = bias_ref[pl.ds(pl.multiple_of(brow, BLK), BLK), :]
            res = []
            for sel in (low, ~low):
                qm = jnp.where(sel, q2, jnp.zeros_like(q2))
                sc = jax.lax.dot_general(qm, k2, nt, preferred_element_type=F32) + bias
                m = jnp.max(sc, axis=-1, keepdims=True)
                p = jnp.exp2(sc - m)
                l = jnp.sum(p, axis=-1, keepdims=True)
                o = jnp.dot(p.astype(BF16), v2, preferred_element_type=F32)
                res.append((o, m, l))
            o_b, m_b, l_b = (jnp.where(low, x0, x1) for x0, x1 in zip(*res))
            if natural:
                for g in range(BLK // SUBLANES):
                    rows = pl.ds((g % 2) * SUBLANES * pr + (BLK // FOLD) * j + g // 2, SUBLANES, stride=pr)
                    part = slice(g * SUBLANES, (g + 1) * SUBLANES)
                    o_acc[rows, :] = o_b[part]
                    m_acc[rows, :] = m_b[part]
                    l_acc[rows, :] = l_b[part]
                return c
            for a in range(nch):
                rows = pl.ds(pl.multiple_of(rd * pr + qs * j + a * dil * pr, SUBLANES), qs)
                part = slice(a * qs, (a + 1) * qs)
                m_old = m_acc[rows, :]
                m_new = jnp.maximum(m_old, m_b[part])
                keep = m_old >= m_b[part]
                e_min = jnp.exp2(jnp.minimum(m_old, m_b[part]) - m_new)
                e_old = jnp.where(keep, 1.0, e_min)
                e_b = jnp.where(keep, e_min, 1.0)
                o_new = o_acc[rows, :] * e_old + o_b[part] * e_b
                l_new = l_acc[rows, :] * e_old + l_b[part] * e_b
                if last:
                    o_acc[rows, :] = o_new * (1.0 / l_new)
                else:
                    o_acc[rows, :] = o_new
                    m_acc[rows, :] = m_new
                    l_acc[rows, :] = l_new
            return c

        jax.lax.fori_loop(0, dil * nblk, block, 0, unroll=BLOCK_UNROLL)

    def unfold(mi, c):
        rows = o_acc[pl.ds(mi, FOLD, stride=pr), :]
        o_ref[0, pl.ds(pl.multiple_of(mi * FOLD, FOLD), FOLD), :] = rows.astype(BF16)
        return c

    jax.lax.fori_loop(0, s // FOLD, unfold, 0, unroll=4)


def _attn_prompt(q, k, v, b):
    npairs, n, _ = q.shape
    s, d = n // b, npairs * LANES
    q, k, v = (a.reshape(npairs * b, s, LANES) for a in (q, k, v))
    assert s % (FOLD * BLK) == 0 and s // FOLD >= 2 * BLK and all(FOLD % dil == 0 for _, dil in DILATED)
    assert DILATED[0][1] == 1 and FOLD == 2 * SUBLANES
    bias = np.concatenate([_branch_bias(FOLD if dil == 1 else dil) for _, dil in DILATED], axis=0)
    seq = pl.BlockSpec((1, s, LANES), lambda bi, hp: (hp * b + bi, 0, 0))
    folded = pltpu.VMEM((FOLD * (s // FOLD + 2 * SUBLANES), LANES), BF16)
    acc = pltpu.VMEM((FOLD * _fold_rows(s), LANES), F32)
    return pl.pallas_call(
        _attn_prompt_kernel,
        grid=(b, d // LANES),
        in_specs=[seq, seq, seq, _resident(bias.shape)],
        out_specs=pl.BlockSpec((1, s, LANES), lambda bi, hp: (bi, 0, hp)),
        out_shape=jax.ShapeDtypeStruct((b, s, d), BF16),
        scratch_shapes=[folded] * 3 + [acc] * 3,
        compiler_params=pltpu.CompilerParams(dimension_semantics=("arbitrary", "arbitrary"),
                                             vmem_limit_bytes=VMEM_LIMIT_ATTN),
        name="attn_prompt",
    )(q, k, v, jnp.asarray(bias))


def _branch_counts(t, w_buf):
    i = np.arange(t)[:, None]
    c = np.arange(w_buf)[None, :]
    n = np.arange(LANES)[None, :]
    cnt_c = np.zeros((t, w_buf), np.float32)
    cnt_n = np.zeros((t, LANES), np.float32)
    for win, dil in DILATED:
        dist = w_buf + i - c
        cnt_c += ((dist % dil == 0) & (dist >= dil) & (dist <= win)).astype(np.float32)
        dist = i - n
        cnt_n += ((n < t) & (dist >= 0) & (dist % dil == 0) & (dist <= win)).astype(np.float32)
    return np.concatenate([cnt_c, cnt_n], axis=1)


def _attn_sample_kernel(q_ref, kn_ref, vn_ref, kc_ref, vc_ref, cnt_ref, o_ref, knp, vnp, *, t):
    rows = 2 * SUBLANES

    @pl.when(pl.program_id(0) == 0)
    def _():
        knp[...] = jnp.zeros_like(knp)
        vnp[...] = jnp.zeros_like(vnp)

    knp[:, 0:t] = kn_ref[0].T
    vnp[:, 0:t] = vn_ref[0].T
    cnt = cnt_ref[...]
    valid = cnt > 0.0
    q = jnp.concatenate([q_ref[0], jnp.zeros((rows - t, D_ATTN), F32)], axis=0).astype(BF16)
    ones = jnp.ones((SUBLANES, cnt.shape[1]), F32)
    nt = (((1,), (1,)), ((), ()))
    outs = []
    for h in range(N_HEADS):
        hrows = slice(h * HEAD_DIM, (h + 1) * HEAD_DIM)
        kt = jnp.concatenate([kc_ref[0, h], knp[hrows, :]], axis=1).astype(BF16)
        sc = jnp.dot(q[:, hrows], kt, preferred_element_type=F32)
        sc = jnp.where(valid, sc, NEG_INF)
        m = jnp.maximum(jnp.max(sc, axis=-1, keepdims=True), -1e30)
        p = (cnt * jnp.exp(sc - m)).astype(BF16)
        vt = jnp.concatenate([vc_ref[0, h], vnp[hrows, :]], axis=1)
        vt = jnp.concatenate([vt, ones], axis=0).astype(BF16)
        o = jax.lax.dot_general(vt, p, nt, preferred_element_type=F32)
        outs.append((o[0:HEAD_DIM] * (1.0 / o[HEAD_DIM:HEAD_DIM + 1]))[:, 0:t])
    o_ref[0] = jnp.concatenate(outs, axis=0).T


def _attn_sample(q, k_new, v_new, k_cache_t, v_cache_t):
    b, t, d = q.shape
    w = k_cache_t.shape[3]
    assert t <= SUBLANES and w % LANES == 0
    cnt = np.zeros((2 * SUBLANES, w + LANES), np.float32)
    cnt[:t] = _branch_counts(t, w)
    tok = pl.BlockSpec((1, t, d), lambda bi: (bi, 0, 0))
    cache = pl.BlockSpec((1,) + k_cache_t.shape[1:], lambda bi: (bi, 0, 0, 0))
    return pl.pallas_call(
        functools.partial(_attn_sample_kernel, t=t),
        grid=(b,),
        in_specs=[tok, tok, tok, cache, cache, _resident(cnt.shape)],
        out_specs=tok,
        out_shape=jax.ShapeDtypeStruct((b, t, d), F32),
        scratch_shapes=[pltpu.VMEM((d, LANES), F32), pltpu.VMEM((d, LANES), F32)],
        compiler_params=pltpu.CompilerParams(dimension_semantics=("arbitrary",),
                                             vmem_limit_bytes=VMEM_LIMIT),
        name="attn_sample",
    )(q, k_new, v_new, k_cache_t, v_cache_t, jnp.asarray(cnt))


def _rglru_coeffs(xc, wa_ref, ba_ref, wx_ref, bx_ref, lam_ref):
    xb = xc.astype(BF16)
    rg = jax.nn.sigmoid(jnp.dot(xb, wa_ref[...], preferred_element_type=F32) + ba_ref[...])
    ig = jax.nn.sigmoid(jnp.dot(xb, wx_ref[...], preferred_element_type=F32) + bx_ref[...])
    log_a = (-LRU_C * jax.nn.softplus(-lam_ref[...])) * rg
    a = jnp.exp(log_a)
    return a, jnp.sqrt(-jnp.tanh(log_a) * (a * a + 1.0)) * ig * xc


def _rglru_tile(u, g, tail, h_prev, cw_ref, cb_ref, wa_ref, ba_ref, wx_ref, bx_ref, lam_ref):
    tc = u.shape[0]
    pad = SUBLANES
    full = jnp.concatenate([tail, u], axis=0)
    first = pad - (CONV_WIDTH - 1)

    def tap(j):
        return pltpu.roll(full, (pad + tc - first - j) % (pad + tc), axis=0)[0:tc]

    xc = cb_ref[...] + cw_ref[0:1, :] * tap(0)
    for j in range(1, CONV_WIDTH):
        xc = xc + cw_ref[j:j + 1, :] * tap(j)
    new_tail = full[tc:tc + pad]
    a, b = _rglru_coeffs(xc, wa_ref, ba_ref, wx_ref, bx_ref, lam_ref)

    row = jax.lax.broadcasted_iota(jnp.int32, (SUBLANES, u.shape[1]), 0)
    h = h_prev
    hs = []
    for gi in range(tc // SUBLANES):
        ag = a[gi * SUBLANES:(gi + 1) * SUBLANES]
        bg = b[gi * SUBLANES:(gi + 1) * SUBLANES]
        for sh in (1, 2, 4):
            a_prev = pltpu.roll(ag, sh, axis=0)
            b_prev = pltpu.roll(bg, sh, axis=0)
            take = row >= sh
            bg = jnp.where(take, ag * b_prev + bg, bg)
            ag = jnp.where(take, ag * a_prev, ag)
        hg = ag * h + bg
        hs.append(hg)
        h = hg[SUBLANES - 1:SUBLANES, :]
    r = (jnp.concatenate(hs, axis=0) * jax.nn.gelu(g)).astype(BF16)
    return r, new_tail, h


def _rglru_steps_kernel(u_ref, g_ref, cbuf_ref, h0_ref, cw_ref, cb_ref, wa_ref, ba_ref, wx_ref, bx_ref,
                        lam_ref, r_ref, hlast_ref, tail_ref, *, t):
    nb = h0_ref.shape[0]
    taps = CONV_WIDTH - 1
    step_rows = lambda ti: pl.ds(ti, nb, stride=t)
    full = [cbuf_ref[j] for j in range(taps)] + [u_ref[step_rows(ti), :] for ti in range(t)]
    xcs = []
    for ti in range(t):
        xc = cb_ref[...] + cw_ref[0:1, :] * full[ti]
        for j in range(1, CONV_WIDTH):
            xc = xc + cw_ref[j:j + 1, :] * full[ti + j]
        xcs.append(xc)
    a, b = _rglru_coeffs(jnp.concatenate(xcs, axis=0), wa_ref, ba_ref, wx_ref, bx_ref, lam_ref)
    h = h0_ref[...]
    for ti in range(t):
        rows = slice(ti * nb, (ti + 1) * nb)
        h = a[rows] * h + b[rows]
        r_ref[step_rows(ti), :] = h * jax.nn.gelu(g_ref[step_rows(ti), :])
    hlast_ref[...] = h
    for j in range(taps):
        tail_ref[j] = full[t + j]


def _rglru_steps(u, g, conv_buf, h0, conv_w, conv_b, wa_bd, b_a, wx_bd, b_x, lam, *, t):
    n, r = u.shape
    nb = n // t
    taps = CONV_WIDTH - 1
    assert t >= taps
    col = lambda rows: pl.BlockSpec((rows, LANES), lambda c: (0, c))
    state = pl.BlockSpec((taps, nb, LANES), lambda c: (0, 0, c))
    diag = pl.BlockSpec((LANES, LANES), lambda c: (c, c))
    out, h_last, tail = pl.pallas_call(
        functools.partial(_rglru_steps_kernel, t=t),
        grid=(r // LANES,),
        in_specs=[col(n), col(n), state, col(nb), col(CONV_WIDTH), col(1), diag, col(1), diag, col(1), col(1)],
        out_specs=[col(n), col(nb), state],
        out_shape=[jax.ShapeDtypeStruct((n, r), F32), jax.ShapeDtypeStruct((nb, r), F32),
                   jax.ShapeDtypeStruct((taps, nb, r), F32)],
        compiler_params=pltpu.CompilerParams(dimension_semantics=("arbitrary",),
                                             vmem_limit_bytes=VMEM_LIMIT),
        name="rglru_steps",
    )(u, g, jnp.swapaxes(conv_buf, 0, 1), h0, conv_w, conv_b.reshape(1, r), wa_bd, b_a.reshape(1, r),
      wx_bd, b_x.reshape(1, r), lam.reshape(1, r))
    return out, h_last, jnp.swapaxes(tail, 0, 1)


def _out_ffn_kernel(x1_ref, attn_ref, r_ref, wo_ref, ln2_ref, wg_ref, wu_ref, wd_ref, lnf_ref, y_ref,
                    *, final_norm):
    mix = jnp.concatenate([attn_ref[...].astype(BF16), r_ref[...].astype(BF16)], axis=-1)
    x2 = x1_ref[...] + jnp.dot(mix, wo_ref[...], preferred_element_type=F32)
    xn = _rms(x2, ln2_ref[...]).astype(BF16)
    x3 = x2 + 0.5 * _swiglu(xn, wg_ref, wu_ref, wd_ref)
    y_ref[...] = _rms(x3, lnf_ref[...]) if final_norm else x3


def _out_ffn(x1, attn, r, wo, ln2, wg, wu, wd, lnf, *, final_norm, tm):
    n, dm = x1.shape
    dff = wg.shape[1]
    row = lambda w: pl.BlockSpec((tm, w), lambda i: (i, 0))
    return pl.pallas_call(
        functools.partial(_out_ffn_kernel, final_norm=final_norm),
        grid=(n // tm,),
        in_specs=[row(dm), row(D_ATTN), row(r.shape[1]), _resident(wo.shape), _resident((1, dm)),
                  _resident((dm, dff)), _resident((dm, dff)), _resident((dff, dm)), _resident((1, dm))],
        out_specs=row(dm),
        out_shape=jax.ShapeDtypeStruct((n, dm), F32),
        compiler_params=pltpu.CompilerParams(dimension_semantics=("arbitrary",),
                                             vmem_limit_bytes=VMEM_LIMIT),
        name="out_ffn",
    )(x1, attn, r, wo, ln2, wg, wu, wd, lnf)


def _block_diag(w):
    nb, c, d = w.shape
    return jnp.einsum("ncd,nm->ncmd", w, jnp.eye(nb, dtype=w.dtype)).reshape(nb * c, nb * d)


def _layer(x, conv_buf, h0, k_past, v_past, lw, lnf, *, final_norm, tm, tm_out):
    (ln1, w1g, w1u, w1d, ln_m, w_in, conv_w, conv_b, wa_bd, b_a, wx_bd, b_x, lam,
     w_out, ln2, w2g, w2u, w2d) = lw
    b, t, dm = x.shape
    r = conv_w.shape[-1]
    prompt = k_past is None
    keep = min(MAX_WINDOW, t) if prompt else t
    shp = lambda a: a.reshape(b, t, a.shape[-1])
    rnn_w = (conv_w, conv_b, wa_bd, b_a, wx_bd, b_x, lam)
    if prompt:
        x1, q, k, v, kt, vt, rnn, h_last, new_buf = _ffn_in_rglru(
            x.reshape(b * t, dm), ln1, w1g, w1u, w1d, ln_m, w_in, conv_buf, h0, *rnn_w, seq=t, keep=keep, tm=tm)
        attn = _attn_prompt(q, k, v, b)
        k_state = jnp.transpose(kt.reshape(b, N_HEADS, HEAD_DIM, keep), (0, 3, 1, 2))
        v_state = jnp.transpose(vt.reshape(b, N_HEADS, HEAD_DIM, keep), (0, 3, 1, 2))
    else:
        x1, q, k, v, u, g = _ffn_in(x.reshape(b * t, dm), ln1, w1g, w1u, w1d, ln_m, w_in, tm=tm)
        attn = _attn_sample(shp(q), shp(k), shp(v),
                            jnp.transpose(k_past, (0, 2, 3, 1)), jnp.transpose(v_past, (0, 2, 3, 1)))
        rnn, h_last, new_buf = _rglru_steps(u, g, conv_buf, h0, *rnn_w, t=t)
        k_state = shp(k).reshape(b, t, N_HEADS, HEAD_DIM)
        v_state = shp(v).reshape(b, t, N_HEADS, HEAD_DIM)
    y = _out_ffn(x1, attn.reshape(b * t, D_ATTN), rnn.reshape(b * t, r), w_out, ln2, w2g, w2u, w2d, lnf,
                 final_norm=final_norm, tm=tm_out)
    return y.reshape(b, t, dm), k_state, v_state, h_last, new_buf


def kernel(x_prompt, x_sample, cache_k_win, cache_v_win, state_lru_h, state_lru_conv, ln_ffn1, w_ffn1_gate, w_ffn1_up, w_ffn1_down, ln_mix, w_in, conv_w, conv_b, w_gate_a, b_gate_a, w_gate_x, b_gate_x, lru_lambda, w_out, ln_ffn2, w_ffn2_gate, w_ffn2_up, w_ffn2_down, ln_final):
    depth = ln_ffn1.shape[0]
    dm = x_prompt.shape[-1]
    d_rnn = conv_w.shape[-1]
    bp = x_prompt.shape[0]
    xp, xs = x_prompt, x_sample
    outs = [[] for _ in range(8)]
    lnf = ln_final.reshape(1, dm)
    cfg = dict(tm=256, tm_out=512)
    for l in range(depth):
        lw = (ln_ffn1[l].reshape(1, dm), w_ffn1_gate[l].astype(BF16), w_ffn1_up[l].astype(BF16),
              w_ffn1_down[l].astype(BF16), ln_mix[l].reshape(1, dm), w_in[l].astype(BF16),
              conv_w[l], conv_b[l], _block_diag(w_gate_a[l]).astype(BF16), b_gate_a[l].reshape(-1),
              _block_diag(w_gate_x[l]).astype(BF16), b_gate_x[l].reshape(-1), lru_lambda[l],
              w_out[l].astype(BF16), ln_ffn2[l].reshape(1, dm), w_ffn2_gate[l].astype(BF16),
              w_ffn2_up[l].astype(BF16), w_ffn2_down[l].astype(BF16))
        last = l == depth - 1
        zero_buf = jnp.zeros((bp, CONV_WIDTH - 1, d_rnn), xp.dtype)
        zero_h = jnp.zeros((bp, d_rnn), xp.dtype)
        xp, kp, vp, hp, cp = _layer(xp, zero_buf, zero_h, None, None, lw, lnf, final_norm=last, **cfg)
        xs, kn, vn, hn, cn = _layer(xs, state_lru_conv[l], state_lru_h[l], cache_k_win[l],
                                    cache_v_win[l], lw, lnf, final_norm=last, **cfg)
        for lst, val in zip(outs, (kp, vp, hp, cp, kn, vn, hn, cn)):
            lst.append(val)
    return (xp, xs) + tuple(jnp.stack(lst) for lst in outs)
```

```python
import functools

import jax
import jax.numpy as jnp
import numpy as np
from jax.experimental import pallas as pl
from jax.experimental.pallas import tpu as pltpu

N_HEADS = 8
HEAD_DIM = 64
D_ATTN = N_HEADS * HEAD_DIM
CONV_WIDTH = 4
LRU_C = 8.0
DILATED = ((128, 1), (512, 4), (2048, 16))
MAX_WINDOW = 2048
BLK = 128
EPS = 1e-6

LANES = 128
SUBLANES = 8
VMEM_LIMIT = 56 * 1024 * 1024
VMEM_LIMIT_ATTN = 60 * 1024 * 1024

F32 = jnp.float32
BF16 = jnp.bfloat16
NEG_INF = float("-inf")


def _rms(x, g):
    return x * jax.lax.rsqrt(jnp.mean(x * x, axis=-1, keepdims=True) + EPS) * g


def _swiglu(xn, wg_ref, wu_ref, wd_ref):
    gate = jnp.dot(xn, wg_ref[...], preferred_element_type=F32)
    up = jnp.dot(xn, wu_ref[...], preferred_element_type=F32)
    h = (gate * jax.nn.sigmoid(gate) * up).astype(BF16)
    return jnp.dot(h, wd_ref[...], preferred_element_type=F32)


def _resident(shape):
    return pl.BlockSpec(shape, lambda *_: (0,) * len(shape), pipeline_mode=pl.Buffered(1))


def _ffn_in_body(x_ref, ln1_ref, wg_ref, wu_ref, wd_ref, lnm_ref, win_ref, x1_ref, q_ref, k_ref, v_ref):
    x = x_ref[...]
    xn = _rms(x, ln1_ref[...]).astype(BF16)
    x1 = x + 0.5 * _swiglu(xn, wg_ref, wu_ref, wd_ref)
    x1_ref[...] = x1
    zn = _rms(x1, lnm_ref[...]).astype(BF16)
    z = jnp.dot(zn, win_ref[...], preferred_element_type=F32)
    d = D_ATTN
    for i, ref in enumerate((q_ref, k_ref, v_ref)):
        t = z[:, i * d:(i + 1) * d] * (HEAD_DIM ** -0.5) if i == 0 else z[:, i * d:(i + 1) * d]
        if len(ref.shape) == 2:
            ref[...] = t
        else:
            for hp in range(ref.shape[0]):
                ref[hp] = t[:, hp * LANES:(hp + 1) * LANES]
    r = (z.shape[1] - 3 * d) // 2
    return z[:, 3 * d:3 * d + r], z[:, 3 * d + r:]


def _ffn_in_kernel(x_ref, ln1_ref, wg_ref, wu_ref, wd_ref, lnm_ref, win_ref,
                   x1_ref, q_ref, k_ref, v_ref, u_ref, g_ref):
    u, g = _ffn_in_body(x_ref, ln1_ref, wg_ref, wu_ref, wd_ref, lnm_ref, win_ref, x1_ref, q_ref, k_ref, v_ref)
    u_ref[...] = u
    g_ref[...] = g


def _ffn_in_rglru_kernel(x_ref, ln1_ref, wg_ref, wu_ref, wd_ref, lnm_ref, win_ref,
                         cbuf_ref, h0_ref, cw_ref, cb_ref, wa_ref, ba_ref, wx_ref, bx_ref, lam_ref,
                         x1_ref, q_ref, k_ref, v_ref, kt_ref, vt_ref, r_ref, hlast_ref, tail_ref,
                         u_s, g_s, tail_s, hcar, *, tiles_per_seq):
    i = pl.program_id(0)

    @pl.when(i == 0)
    def _():
        for ref in (u_s, g_s, tail_s, hcar):
            ref[...] = jnp.zeros_like(ref)

    starts = (i - 1) % tiles_per_seq == 0
    tail = jnp.where(starts, cbuf_ref[0], tail_s[...])
    h_prev = jnp.where(starts, h0_ref[0], hcar[...])
    r, new_tail, h = _rglru_tile(u_s[...], g_s[...], tail, h_prev, cw_ref, cb_ref, wa_ref, ba_ref,
                                 wx_ref, bx_ref, lam_ref)
    u, g = _ffn_in_body(x_ref, ln1_ref, wg_ref, wu_ref, wd_ref, lnm_ref, win_ref, x1_ref, q_ref, k_ref, v_ref)
    for hp in range(k_ref.shape[0]):
        kt_ref[0, hp * LANES:(hp + 1) * LANES, :] = k_ref[hp].T
        vt_ref[0, hp * LANES:(hp + 1) * LANES, :] = v_ref[hp].T
    tail_s[...] = new_tail
    hcar[...] = h
    r_ref[...] = r
    hlast_ref[0] = h
    tail_ref[0] = new_tail
    u_s[...] = u
    g_s[...] = g


def _ffn_in(x, ln1, wg, wu, wd, lnm, win, *, tm):
    n, dm = x.shape
    dff = wg.shape[1]
    d_rnn = (win.shape[1] - 3 * D_ATTN) // 2
    row = lambda w: pl.BlockSpec((tm, w), lambda i: (i, 0))
    widths = (dm, D_ATTN, D_ATTN, D_ATTN, d_rnn, d_rnn)
    return pl.pallas_call(
        _ffn_in_kernel,
        grid=(n // tm,),
        in_specs=[row(dm), _resident((1, dm)), _resident((dm, dff)), _resident((dm, dff)),
                  _resident((dff, dm)), _resident((1, dm)), _resident(win.shape)],
        out_specs=[row(w) for w in widths],
        out_shape=[jax.ShapeDtypeStruct((n, w), F32) for w in widths],
        compiler_params=pltpu.CompilerParams(dimension_semantics=("arbitrary",),
                                             vmem_limit_bytes=VMEM_LIMIT),
        name="ffn_in",
    )(x, ln1, wg, wu, wd, lnm, win)


def _ffn_in_rglru(x, ln1, wg, wu, wd, lnm, win, conv_buf, h0, conv_w, conv_b, wa_bd, b_a, wx_bd, b_x, lam,
                  *, seq, keep, tm):
    n, dm = x.shape
    dff = wg.shape[1]
    r = conv_w.shape[-1]
    nb = n // seq
    tiles = n // tm
    tps = seq // tm
    first_kept = tps - keep // tm
    pad = SUBLANES
    cbuf = jnp.pad(conv_buf, ((0, 0), (pad - (CONV_WIDTH - 1), 0), (0, 0)))
    cur_tile = lambda i: jnp.minimum(i, tiles - 1)
    cur = lambda w: pl.BlockSpec((tm, w), lambda i: (cur_tile(i), 0))
    kept = pl.BlockSpec((1, D_ATTN, tm), lambda i: (cur_tile(i) // tps, 0,
                                                    jnp.maximum(cur_tile(i) % tps - first_kept, 0)))
    npairs = D_ATTN // LANES
    pairs = pl.BlockSpec((npairs, tm, LANES), lambda i: (0, cur_tile(i), 0))
    prev = lambda w: pl.BlockSpec((tm, w), lambda i: (jnp.maximum(i - 1, 0), 0))
    prev_seq = lambda rows: pl.BlockSpec((1, rows, r), lambda i: (jnp.maximum(i - 1, 0) // tps, 0, 0))
    vec = _resident((1, r))
    x1, q, k, v, kt, vt, rnn, h_last, tail = pl.pallas_call(
        functools.partial(_ffn_in_rglru_kernel, tiles_per_seq=tps),
        grid=(tiles + 1,),
        in_specs=[cur(dm), _resident((1, dm)), _resident((dm, dff)), _resident((dm, dff)),
                  _resident((dff, dm)), _resident((1, dm)), _resident(win.shape),
                  prev_seq(pad), prev_seq(1), _resident((CONV_WIDTH, r)), vec,
                  _resident((r, r)), vec, _resident((r, r)), vec, vec],
        out_specs=[cur(dm), pairs, pairs, pairs, kept, kept,
                   prev(r), prev_seq(1), prev_seq(pad)],
        out_shape=[jax.ShapeDtypeStruct((n, dm), F32)] + [jax.ShapeDtypeStruct((npairs, n, LANES), F32)] * 3
                  + [jax.ShapeDtypeStruct((nb, D_ATTN, keep), F32)] * 2
                  + [jax.ShapeDtypeStruct((n, r), BF16), jax.ShapeDtypeStruct((nb, 1, r), F32),
                     jax.ShapeDtypeStruct((nb, pad, r), F32)],
        scratch_shapes=[pltpu.VMEM((tm, r), F32), pltpu.VMEM((tm, r), F32),
                        pltpu.VMEM((pad, r), F32), pltpu.VMEM((1, r), F32)],
        compiler_params=pltpu.CompilerParams(dimension_semantics=("arbitrary",),
                                             vmem_limit_bytes=VMEM_LIMIT),
        name="ffn_in_rglru",
    )(x, ln1, wg, wu, wd, lnm, win, cbuf, h0.reshape(nb, 1, r), conv_w, conv_b.reshape(1, r),
      wa_bd, b_a.reshape(1, r), wx_bd, b_x.reshape(1, r), lam.reshape(1, r))
    return x1, q, k, v, kt, vt, rnn, h_last.reshape(nb, r), tail[:, pad - (CONV_WIDTH - 1):]


FOLD = 16
BLOCK_UNROLL = 32


def _fold_rows(s):
    return s // FOLD + SUBLANES


def _branch_bias(dil):
    nch = FOLD // dil
    qs = BLK // nch
    ip = np.arange(BLK)
    kp = np.arange(2 * BLK)
    i = nch * (ip % qs) + ip // qs
    kk = nch * (kp % (2 * qs)) + kp // (2 * qs)
    delta = kk[None, :] - i[:, None]
    first = delta <= 0
    other = (delta >= 0) & (delta <= BLK)
    return np.where(np.concatenate([first, other], axis=0), 0.0, NEG_INF).astype(np.float32)


def _attn_prompt_kernel(q_ref, k_ref, v_ref, bias_ref, o_ref, qf, kf, vf, o_acc, m_acc, l_acc):
    s = q_ref.shape[1]
    pr = o_acc.shape[0] // FOLD
    prb = qf.shape[0] // FOLD
    tile = 2 * SUBLANES
    nt = (((1,), (1,)), ((), ()))
    low = jax.lax.broadcasted_iota(jnp.int32, (BLK, LANES), 1) < HEAD_DIM
    log2e = float(np.log2(np.e))

    def fold(mi, c):
        for grp in range(FOLD // SUBLANES):
            src = pl.ds(pl.multiple_of(mi * FOLD + grp * SUBLANES, SUBLANES), SUBLANES)
            dst = pl.ds(mi + grp * SUBLANES * pr, SUBLANES, stride=pr)
            o_acc[dst, :] = q_ref[0, src, :]
            m_acc[dst, :] = k_ref[0, src, :]
            l_acc[dst, :] = v_ref[0, src, :]
        return c

    jax.lax.fori_loop(0, s // FOLD, fold, 0, unroll=8)

    def pack(r, c):
        for ch in range(s // FOLD // BLK):
            src = pl.ds(pl.multiple_of(r * pr + ch * BLK, SUBLANES), BLK)
            dst = pl.ds(pl.multiple_of(r * prb + ch * BLK, tile), BLK)
            qf[dst, :] = (o_acc[src, :] * log2e).astype(BF16)
            kf[dst, :] = m_acc[src, :].astype(BF16)
            vf[dst, :] = l_acc[src, :].astype(BF16)
        return c

    jax.lax.fori_loop(0, FOLD, pack, 0)

    n_branches = len(DILATED)
    for bi, (_, dil) in enumerate(DILATED):
        natural = dil == 1
        nch = 1 if natural else FOLD // dil
        qs = BLK // nch
        nblk = s // (dil * BLK)
        first, last = bi == 0, bi == n_branches - 1
        assert first or not natural

        def block(idx, c, bi=bi, dil=dil, natural=natural, nch=nch, qs=qs, nblk=nblk, first=first, last=last):
            rd = idx // nblk
            j = idx % nblk
            prev = jnp.maximum(j - 1, 0)

            def gather(ref, base, n):
                if natural:
                    return ref[0, pl.ds(pl.multiple_of(base, BLK), n), :]
                return jnp.concatenate(
                    [ref[pl.ds(pl.multiple_of(rd * prb + base + a * dil * prb, tile), n), :] for a in range(nch)],
                    axis=0)

            if natural:
                q2 = (gather(q_ref, qs * j, qs) * log2e).astype(BF16)
                k2 = gather(k_ref, qs * prev, 2 * qs).astype(BF16)
                v2 = gather(v_ref, qs * prev, 2 * qs).astype(BF16)
            else:
                q2 = gather(qf, qs * j, qs)
                k2 = gather(kf, qs * prev, 2 * qs)
                v2 = gather(vf, qs * prev, 2 * qs)
            brow = (2 * bi + jnp.minimum(j, 1)) * BLK
            bias = bias_ref[pl.ds(pl.multiple_of(brow, BLK), BLK), :]
            res = []
            for sel in (low, ~low):
                qm = jnp.where(sel, q2, jnp.zeros_like(q2))
                sc = jax.lax.dot_general(qm, k2, nt, preferred_element_type=F32) + bias
                m = jnp.max(sc, axis=-1, keepdims=True)
                p = jnp.exp2(sc - m)
                l = jnp.sum(p, axis=-1, keepdims=True)
                o = jnp.dot(p.astype(BF16), v2, preferred_element_type=F32)
                res.append((o, m, l))
            o_b, m_b, l_b = (jnp.where(low, x0, x1) for x0, x1 in zip(*res))
            if natural:
                for g in range(BLK // SUBLANES):
                    rows = pl.ds((g % 2) * SUBLANES * pr + (BLK // FOLD) * j + g // 2, SUBLANES, stride=pr)
                    part = slice(g * SUBLANES, (g + 1) * SUBLANES)
                    o_acc[rows, :] = o_b[part]
                    m_acc[rows, :] = m_b[part]
                    l_acc[rows, :] = l_b[part]
                return c
            for a in range(nch):
                rows = pl.ds(pl.multiple_of(rd * pr + qs * j + a * dil * pr, SUBLANES), qs)
                part = slice(a * qs, (a + 1) * qs)
                m_old = m_acc[rows, :]
                m_new = jnp.maximum(m_old, m_b[part])
                keep = m_old >= m_b[part]
                e_min = jnp.exp2(jnp.minimum(m_old, m_b[part]) - m_new)
                e_old = jnp.where(keep, 1.0, e_min)
                e_b = jnp.where(keep, e_min, 1.0)
                o_new = o_acc[rows, :] * e_old + o_b[part] * e_b
                l_new = l_acc[rows, :] * e_old + l_b[part] * e_b
                if last:
                    o_acc[rows, :] = o_new * (1.0 / l_new)
                else:
                    o_acc[rows, :] = o_new
                    m_acc[rows, :] = m_new
                    l_acc[rows, :] = l_new
            return c

        jax.lax.fori_loop(0, dil * nblk, block, 0, unroll=BLOCK_UNROLL)

    def unfold(mi, c):
        rows = o_acc[pl.ds(mi, FOLD, stride=pr), :]
        o_ref[0, pl.ds(pl.multiple_of(mi * FOLD, FOLD), FOLD), :] = rows.astype(BF16)
        return c

    jax.lax.fori_loop(0, s // FOLD, unfold, 0, unroll=4)


def _attn_prompt(q, k, v, b):
    npairs, n, _ = q.shape
    s, d = n // b, npairs * LANES
    q, k, v = (a.reshape(npairs * b, s, LANES) for a in (q, k, v))
    assert s % (FOLD * BLK) == 0 and s // FOLD >= 2 * BLK and all(FOLD % dil == 0 for _, dil in DILATED)
    assert DILATED[0][1] == 1 and FOLD == 2 * SUBLANES
    bias = np.concatenate([_branch_bias(FOLD if dil == 1 else dil) for _, dil in DILATED], axis=0)
    seq = pl.BlockSpec((1, s, LANES), lambda bi, hp: (hp * b + bi, 0, 0))
    folded = pltpu.VMEM((FOLD * (s // FOLD + 2 * SUBLANES), LANES), BF16)
    acc = pltpu.VMEM((FOLD * _fold_rows(s), LANES), F32)
    return pl.pallas_call(
        _attn_prompt_kernel,
        grid=(b, d // LANES),
        in_specs=[seq, seq, seq, _resident(bias.shape)],
        out_specs=pl.BlockSpec((1, s, LANES), lambda bi, hp: (bi, 0, hp)),
        out_shape=jax.ShapeDtypeStruct((b, s, d), BF16),
        scratch_shapes=[folded] * 3 + [acc] * 3,
        compiler_params=pltpu.CompilerParams(dimension_semantics=("arbitrary", "arbitrary"),
                                             vmem_limit_bytes=VMEM_LIMIT_ATTN),
        name="attn_prompt",
    )(q, k, v, jnp.asarray(bias))


CACHE_SLOTS = 3


def _branch_counts(t, w_buf):
    i = np.arange(t)[:, None]
    c = np.arange(w_buf)[None, :]
    n = np.arange(LANES)[None, :]
    cnt_c = np.zeros((t, w_buf), np.float32)
    cnt_n = np.zeros((t, LANES), np.float32)
    for win, dil in DILATED:
        dist = w_buf + i - c
        cnt_c += ((dist % dil == 0) & (dist >= dil) & (dist <= win)).astype(np.float32)
        dist = i - n
        cnt_n += ((n < t) & (dist >= 0) & (dist % dil == 0) & (dist <= win)).astype(np.float32)
    return np.concatenate([cnt_c, cnt_n], axis=1)


def _attn_sample_kernel(q_ref, kn_ref, vn_ref, kc_hbm, vc_hbm, cnt_ref, o_ref, knp, vnp, kbuf, vbuf, sems, *, t):
    rows = 2 * SUBLANES
    step = pl.program_id(0)
    depth = kbuf.shape[0]

    def copies(s):
        slot = s % depth
        return (pltpu.make_async_copy(kc_hbm.at[s], kbuf.at[slot], sems.at[0, slot]),
                pltpu.make_async_copy(vc_hbm.at[s], vbuf.at[slot], sems.at[1, slot]))

    @pl.when(step == 0)
    def _():
        knp[...] = jnp.zeros_like(knp)
        vnp[...] = jnp.zeros_like(vnp)
        for s0 in range(depth - 1):
            for c in copies(s0):
                c.start()

    @pl.when(step + depth - 1 < pl.num_programs(0))
    def _():
        for c in copies(step + depth - 1):
            c.start()

    for c in copies(step):
        c.wait()
    kc_ref = kbuf.at[step % depth]
    vc_ref = vbuf.at[step % depth]

    knp[:, 0:t] = kn_ref[0].T
    vnp[:, 0:t] = vn_ref[0].T
    cnt = cnt_ref[...]
    valid = cnt > 0.0
    q = jnp.concatenate([q_ref[0], jnp.zeros((rows - t, D_ATTN), F32)], axis=0).astype(BF16)
    ones = jnp.ones((SUBLANES, cnt.shape[1]), F32)
    nt = (((1,), (1,)), ((), ()))
    outs = []
    for h in range(N_HEADS):
        hrows = slice(h * HEAD_DIM, (h + 1) * HEAD_DIM)
        kt = jnp.concatenate([kc_ref[h], knp[hrows, :]], axis=1).astype(BF16)
        sc = jnp.dot(q[:, hrows], kt, preferred_element_type=F32)
        sc = jnp.where(valid, sc, NEG_INF)
        m = jnp.maximum(jnp.max(sc, axis=-1, keepdims=True), -1e30)
        p = (cnt * jnp.exp(sc - m)).astype(BF16)
        vt = jnp.concatenate([vc_ref[h], vnp[hrows, :]], axis=1)
        vt = jnp.concatenate([vt, ones], axis=0).astype(BF16)
        o = jax.lax.dot_general(vt, p, nt, preferred_element_type=F32)
        outs.append((o[0:HEAD_DIM] * (1.0 / o[HEAD_DIM:HEAD_DIM + 1]))[:, 0:t])
    o_ref[0] = jnp.concatenate(outs, axis=0).T


def _attn_sample(q, k_new, v_new, k_cache_t, v_cache_t):
    b, t, d = q.shape
    w = k_cache_t.shape[3]
    assert t <= SUBLANES and w % LANES == 0
    cnt = np.zeros((2 * SUBLANES, w + LANES), np.float32)
    cnt[:t] = _branch_counts(t, w)
    tok = pl.BlockSpec((1, t, d), lambda bi: (bi, 0, 0))
    cache = pl.BlockSpec(memory_space=pl.ANY)
    slots = pltpu.VMEM((CACHE_SLOTS,) + k_cache_t.shape[1:], F32)
    assert b >= CACHE_SLOTS
    return pl.pallas_call(
        functools.partial(_attn_sample_kernel, t=t),
        grid=(b,),
        in_specs=[tok, tok, tok, cache, cache, _resident(cnt.shape)],
        out_specs=tok,
        out_shape=jax.ShapeDtypeStruct((b, t, d), F32),
        scratch_shapes=[pltpu.VMEM((d, LANES), F32), pltpu.VMEM((d, LANES), F32), slots, slots,
                        pltpu.SemaphoreType.DMA((2, CACHE_SLOTS))],
        compiler_params=pltpu.CompilerParams(dimension_semantics=("arbitrary",),
                                             vmem_limit_bytes=VMEM_LIMIT),
        name="attn_sample",
    )(q, k_new, v_new, k_cache_t, v_cache_t, jnp.asarray(cnt))


def _rglru_coeffs(xc, wa_ref, ba_ref, wx_ref, bx_ref, lam_ref):
    xb = xc.astype(BF16)
    rg = jax.nn.sigmoid(jnp.dot(xb, wa_ref[...], preferred_element_type=F32) + ba_ref[...])
    ig = jax.nn.sigmoid(jnp.dot(xb, wx_ref[...], preferred_element_type=F32) + bx_ref[...])
    log_a = (-LRU_C * jax.nn.softplus(-lam_ref[...])) * rg
    a = jnp.exp(log_a)
    return a, jnp.sqrt(-jnp.tanh(log_a) * (a * a + 1.0)) * ig * xc


def _rglru_tile(u, g, tail, h_prev, cw_ref, cb_ref, wa_ref, ba_ref, wx_ref, bx_ref, lam_ref):
    tc = u.shape[0]
    pad = SUBLANES
    full = jnp.concatenate([tail, u], axis=0)
    first = pad - (CONV_WIDTH - 1)

    def tap(j):
        return pltpu.roll(full, (pad + tc - first - j) % (pad + tc), axis=0)[0:tc]

    xc = cb_ref[...] + cw_ref[0:1, :] * tap(0)
    for j in range(1, CONV_WIDTH):
        xc = xc + cw_ref[j:j + 1, :] * tap(j)
    new_tail = full[tc:tc + pad]
    a, b = _rglru_coeffs(xc, wa_ref, ba_ref, wx_ref, bx_ref, lam_ref)

    row = jax.lax.broadcasted_iota(jnp.int32, (SUBLANES, u.shape[1]), 0)
    h = h_prev
    hs = []
    for gi in range(tc // SUBLANES):
        ag = a[gi * SUBLANES:(gi + 1) * SUBLANES]
        bg = b[gi * SUBLANES:(gi + 1) * SUBLANES]
        for sh in (1, 2, 4):
            a_prev = pltpu.roll(ag, sh, axis=0)
            b_prev = pltpu.roll(bg, sh, axis=0)
            take = row >= sh
            bg = jnp.where(take, ag * b_prev + bg, bg)
            ag = jnp.where(take, ag * a_prev, ag)
        hg = ag * h + bg
        hs.append(hg)
        h = hg[SUBLANES - 1:SUBLANES, :]
    r = (jnp.concatenate(hs, axis=0) * jax.nn.gelu(g)).astype(BF16)
    return r, new_tail, h


def _rglru_steps_kernel(u_ref, g_ref, cbuf_ref, h0_ref, cw_ref, cb_ref, wa_ref, ba_ref, wx_ref, bx_ref,
                        lam_ref, r_ref, hlast_ref, tail_ref, *, t):
    nb = h0_ref.shape[0]
    taps = CONV_WIDTH - 1
    step_rows = lambda ti: pl.ds(ti, nb, stride=t)
    full = [cbuf_ref[j] for j in range(taps)] + [u_ref[step_rows(ti), :] for ti in range(t)]
    xcs = []
    for ti in range(t):
        xc = cb_ref[...] + cw_ref[0:1, :] * full[ti]
        for j in range(1, CONV_WIDTH):
            xc = xc + cw_ref[j:j + 1, :] * full[ti + j]
        xcs.append(xc)
    a, b = _rglru_coeffs(jnp.concatenate(xcs, axis=0), wa_ref, ba_ref, wx_ref, bx_ref, lam_ref)
    h = h0_ref[...]
    for ti in range(t):
        rows = slice(ti * nb, (ti + 1) * nb)
        h = a[rows] * h + b[rows]
        r_ref[step_rows(ti), :] = h * jax.nn.gelu(g_ref[step_rows(ti), :])
    hlast_ref[...] = h
    for j in range(taps):
        tail_ref[j] = full[t + j]


def _rglru_steps(u, g, conv_buf, h0, conv_w, conv_b, wa_bd, b_a, wx_bd, b_x, lam, *, t):
    n, r = u.shape
    nb = n // t
    taps = CONV_WIDTH - 1
    assert t >= taps
    col = lambda rows: pl.BlockSpec((rows, LANES), lambda c: (0, c))
    state = pl.BlockSpec((taps, nb, LANES), lambda c: (0, 0, c))
    diag = pl.BlockSpec((LANES, LANES), lambda c: (c, c))
    out, h_last, tail = pl.pallas_call(
        functools.partial(_rglru_steps_kernel, t=t),
        grid=(r // LANES,),
        in_specs=[col(n), col(n), state, col(nb), col(CONV_WIDTH), col(1), diag, col(1), diag, col(1), col(1)],
        out_specs=[col(n), col(nb), state],
        out_shape=[jax.ShapeDtypeStruct((n, r), F32), jax.ShapeDtypeStruct((nb, r), F32),
                   jax.ShapeDtypeStruct((taps, nb, r), F32)],
        compiler_params=pltpu.CompilerParams(dimension_semantics=("arbitrary",),
                                             vmem_limit_bytes=VMEM_LIMIT),
        name="rglru_steps",
    )(u, g, jnp.swapaxes(conv_buf, 0, 1), h0, conv_w, conv_b.reshape(1, r), wa_bd, b_a.reshape(1, r),
      wx_bd, b_x.reshape(1, r), lam.reshape(1, r))
    return out, h_last, jnp.swapaxes(tail, 0, 1)


def _out_ffn_kernel(x1_ref, attn_ref, r_ref, wo_ref, ln2_ref, wg_ref, wu_ref, wd_ref, lnf_ref, y_ref,
                    *, final_norm):
    mix = jnp.concatenate([attn_ref[...].astype(BF16), r_ref[...].astype(BF16)], axis=-1)
    x2 = x1_ref[...] + jnp.dot(mix, wo_ref[...], preferred_element_type=F32)
    xn = _rms(x2, ln2_ref[...]).astype(BF16)
    x3 = x2 + 0.5 * _swiglu(xn, wg_ref, wu_ref, wd_ref)
    y_ref[...] = _rms(x3, lnf_ref[...]) if final_norm else x3


def _out_ffn(x1, attn, r, wo, ln2, wg, wu, wd, lnf, *, final_norm, tm):
    n, dm = x1.shape
    dff = wg.shape[1]
    row = lambda w: pl.BlockSpec((tm, w), lambda i: (i, 0))
    return pl.pallas_call(
        functools.partial(_out_ffn_kernel, final_norm=final_norm),
        grid=(n // tm,),
        in_specs=[row(dm), row(D_ATTN), row(r.shape[1]), _resident(wo.shape), _resident((1, dm)),
                  _resident((dm, dff)), _resident((dm, dff)), _resident((dff, dm)), _resident((1, dm))],
        out_specs=row(dm),
        out_shape=jax.ShapeDtypeStruct((n, dm), F32),
        compiler_params=pltpu.CompilerParams(dimension_semantics=("arbitrary",),
                                             vmem_limit_bytes=VMEM_LIMIT),
        name="out_ffn",
    )(x1, attn, r, wo, ln2, wg, wu, wd, lnf)


def _block_diag(w):
    nb, c, d = w.shape
    return jnp.einsum("ncd,nm->ncmd", w, jnp.eye(nb, dtype=w.dtype)).reshape(nb * c, nb * d)


def _layer(x, conv_buf, h0, k_past, v_past, lw, lnf, *, final_norm, tm, tm_out):
    (ln1, w1g, w1u, w1d, ln_m, w_in, conv_w, conv_b, wa_bd, b_a, wx_bd, b_x, lam,
     w_out, ln2, w2g, w2u, w2d) = lw
    b, t, dm = x.shape
    r = conv_w.shape[-1]
    prompt = k_past is None
    keep = min(MAX_WINDOW, t) if prompt else t
    shp = lambda a: a.reshape(b, t, a.shape[-1])
    rnn_w = (conv_w, conv_b, wa_bd, b_a, wx_bd, b_x, lam)
    if prompt:
        x1, q, k, v, kt, vt, rnn, h_last, new_buf = _ffn_in_rglru(
            x.reshape(b * t, dm), ln1, w1g, w1u, w1d, ln_m, w_in, conv_buf, h0, *rnn_w, seq=t, keep=keep, tm=tm)
        attn = _attn_prompt(q, k, v, b)
        k_state = jnp.transpose(kt.reshape(b, N_HEADS, HEAD_DIM, keep), (0, 3, 1, 2))
        v_state = jnp.transpose(vt.reshape(b, N_HEADS, HEAD_DIM, keep), (0, 3, 1, 2))
    else:
        x1, q, k, v, u, g = _ffn_in(x.reshape(b * t, dm), ln1, w1g, w1u, w1d, ln_m, w_in, tm=tm)
        attn = _attn_sample(shp(q), shp(k), shp(v),
                            jnp.transpose(k_past, (0, 2, 3, 1)), jnp.transpose(v_past, (0, 2, 3, 1)))
        rnn, h_last, new_buf = _rglru_steps(u, g, conv_buf, h0, *rnn_w, t=t)
        k_state = shp(k).reshape(b, t, N_HEADS, HEAD_DIM)
        v_state = shp(v).reshape(b, t, N_HEADS, HEAD_DIM)
    y = _out_ffn(x1, attn.reshape(b * t, D_ATTN), rnn.reshape(b * t, r), w_out, ln2, w2g, w2u, w2d, lnf,
                 final_norm=final_norm, tm=tm_out)
    return y.reshape(b, t, dm), k_state, v_state, h_last, new_buf


def kernel(x_prompt, x_sample, cache_k_win, cache_v_win, state_lru_h, state_lru_conv, ln_ffn1, w_ffn1_gate, w_ffn1_up, w_ffn1_down, ln_mix, w_in, conv_w, conv_b, w_gate_a, b_gate_a, w_gate_x, b_gate_x, lru_lambda, w_out, ln_ffn2, w_ffn2_gate, w_ffn2_up, w_ffn2_down, ln_final):
    depth = ln_ffn1.shape[0]
    dm = x_prompt.shape[-1]
    d_rnn = conv_w.shape[-1]
    bp = x_prompt.shape[0]
    xp, xs = x_prompt, x_sample
    outs = [[] for _ in range(8)]
    lnf = ln_final.reshape(1, dm)
    cfg = dict(tm=256, tm_out=512)
    for l in range(depth):
        lw = (ln_ffn1[l].reshape(1, dm), w_ffn1_gate[l].astype(BF16), w_ffn1_up[l].astype(BF16),
              w_ffn1_down[l].astype(BF16), ln_mix[l].reshape(1, dm), w_in[l].astype(BF16),
              conv_w[l], conv_b[l], _block_diag(w_gate_a[l]).astype(BF16), b_gate_a[l].reshape(-1),
              _block_diag(w_gate_x[l]).astype(BF16), b_gate_x[l].reshape(-1), lru_lambda[l],
              w_out[l].astype(BF16), ln_ffn2[l].reshape(1, dm), w_ffn2_gate[l].astype(BF16),
              w_ffn2_up[l].astype(BF16), w_ffn2_down[l].astype(BF16))
        last = l == depth - 1
        zero_buf = jnp.zeros((bp, CONV_WIDTH - 1, d_rnn), xp.dtype)
        zero_h = jnp.zeros((bp, d_rnn), xp.dtype)
        xp, kp, vp, hp, cp = _layer(xp, zero_buf, zero_h, None, None, lw, lnf, final_norm=last, **cfg)
        xs, kn, vn, hn, cn = _layer(xs, state_lru_conv[l], state_lru_h[l], cache_k_win[l],
                                    cache_v_win[l], lw, lnf, final_norm=last, **cfg)
        for lst, val in zip(outs, (kp, vp, hp, cp, kn, vn, hn, cn)):
            lst.append(val)
    return (xp, xs) + tuple(jnp.stack(lst) for lst in outs)
```
